```python
import jax, jax.numpy as jnp
from jax import lax
import numpy as np

D_MODEL = 1024
BATCH = 8
SEQ = 2048
DEPTH = 2
DEC_BATCH = 32
DEC_SEQ = 4
PAST_LEN = 16384
PAGE_SIZE = 128

HEAD_DIM = 64
N_HEADS_A = D_MODEL // HEAD_DIM
N_HEADS_B = D_MODEL // HEAD_DIM
BRANCHES = ((128, 1), (512, 4), (2048, 16))
N_GROUPS = len(BRANCHES)
WINDOW_MAX = max(w for w, _ in BRANCHES)
D_FF = 4 * D_MODEL
N_A_LAYERS = DEPTH // 2
N_B_LAYERS = DEPTH - N_A_LAYERS
BLOCK = 128
ATTN_SCALE = HEAD_DIM ** -0.5
RMS_EPS = 1e-6
SB_BIAS_INIT = -6.0

kernel_name = 'yoco_stickbreak_dilated_window_step'


def _rms_norm(x, g):
    xf = x.astype(jnp.float32)
    y = xf * lax.rsqrt(jnp.mean(xf * xf, axis=-1, keepdims=True) + RMS_EPS)
    return (y * g.astype(jnp.float32)).astype(x.dtype)


def _alibi_slopes(n_heads):
    return 2.0 ** (-8.0 * (jnp.arange(n_heads, dtype=jnp.float32) + 1.0) / n_heads)


def _sb_weights(z, mask):
    ell = jnp.where(mask, jax.nn.log_sigmoid(-z), 0.0)
    rem = lax.cumsum(ell, axis=z.ndim - 1, reverse=True) - ell
    return jnp.where(mask, jnp.exp(jax.nn.log_sigmoid(z) + rem), 0.0)


def _stick_breaking_prompt(q, k, v, bias):
    B, S, H, HD = q.shape
    nb = S // BLOCK
    kpos = jnp.arange(S)
    qb = q.reshape(B, nb, BLOCK, H, HD).swapaxes(0, 1)
    bias = bias.astype(jnp.float32)[None, :, None, None]

    def one_block(args):
        qi, b = args
        qpos = b * BLOCK + jnp.arange(BLOCK)
        z = jnp.einsum('bqhd,bkhd->bhqk', qi, k, preferred_element_type=jnp.float32) * ATTN_SCALE + bias
        a = _sb_weights(z, kpos[None, :] < qpos[:, None])
        return jnp.einsum('bhqk,bkhd->bqhd', a, v)

    out = lax.map(one_block, (qb, jnp.arange(nb)))
    return out.swapaxes(0, 1).reshape(B, S, H, HD)


def _stick_breaking_sample(q, k_new, v_new, k_past, v_past, bias):
    P = k_past.shape[1]
    T = q.shape[1]
    z = jnp.concatenate([
        jnp.einsum('bqhd,bkhd->bhqk', q, k_past, preferred_element_type=jnp.float32),
        jnp.einsum('bqhd,bkhd->bhqk', q, k_new, preferred_element_type=jnp.float32)], axis=-1) * ATTN_SCALE
    z = z + bias.astype(jnp.float32)[None, :, None, None]
    kpos = jnp.arange(P + T)
    qpos = P + jnp.arange(T)
    a = _sb_weights(z, kpos[None, :] < qpos[:, None])
    return (jnp.einsum('bhqk,bkhd->bqhd', a[..., :P], v_past)
            + jnp.einsum('bhqk,bkhd->bqhd', a[..., P:], v_new))


def _dilated_branch_prompt(q, k, v, dil, n_keys, slopes):
    B, S, H, HD = q.shape
    L = S // dil
    nb = -(-L // BLOCK)
    Lp = nb * BLOCK

    def split(x):
        return x.reshape(B, L, dil, H, HD).transpose(0, 2, 1, 3, 4)

    qs = jnp.pad(split(q), ((0, 0), (0, 0), (0, Lp - L), (0, 0), (0, 0)))
    pad_kv = ((0, 0), (0, 0), (BLOCK, Lp - L), (0, 0), (0, 0))
    ks = jnp.pad(split(k), pad_kv)
    vs = jnp.pad(split(v), pad_kv)

    def band(x):
        prev = x[:, :, :Lp].reshape(B, dil, nb, BLOCK, H, HD)
        cur = x[:, :, BLOCK:].reshape(B, dil, nb, BLOCK, H, HD)
        return jnp.concatenate([prev, cur], axis=3)

    kb, vb = band(ks), band(vs)
    qb = qs.reshape(B, dil, nb, BLOCK, H, HD)
    s = jnp.einsum('brnqhd,brnkhd->brnhqk', qb, kb, preferred_element_type=jnp.float32) * ATTN_SCALE
    i = jnp.arange(BLOCK)[:, None]
    j = jnp.arange(2 * BLOCK)[None, :]
    delta = i - j + BLOCK
    kloc = jnp.arange(nb)[:, None, None] * BLOCK + j[None] - BLOCK
    valid = (delta >= 0) & (delta <= n_keys) & (kloc >= 0)
    bias = -slopes[:, None, None] * (delta * dil).astype(jnp.float32)
    s = jnp.where(valid[:, None], s + bias, -jnp.inf)
    lse = jax.nn.logsumexp(s, axis=-1)
    p = jnp.exp(s - lse[..., None])
    o = jnp.einsum('brnhqk,brnkhd->brnqhd', p, vb)
    o = o.reshape(B, dil, Lp, H, HD)[:, :, :L].transpose(0, 2, 1, 3, 4).reshape(B, S, H, HD)
    lse = lse.transpose(0, 1, 2, 4, 3).reshape(B, dil, Lp, H)[:, :, :L]
    lse = lse.transpose(0, 2, 1, 3).reshape(B, S, H)
    return o, lse


def _dilated_branch_sample(q, kk, vv, n_past, dil, n_keys, slopes):
    T = q.shape[1]
    dist = dil * jnp.arange(n_keys + 1)
    kidx = n_past + jnp.arange(T)[:, None] - dist[None, :]
    valid = kidx >= 0
    kidx = jnp.maximum(kidx, 0)
    kg = kk[:, kidx]
    vg = vv[:, kidx]
    s = jnp.einsum('bqhd,bqkhd->bhqk', q, kg, preferred_element_type=jnp.float32) * ATTN_SCALE
    s = jnp.where(valid, s - slopes[:, None, None] * dist.astype(jnp.float32), -jnp.inf)
    lse = jax.nn.logsumexp(s, axis=-1)
    p = jnp.exp(s - lse[..., None])
    o = jnp.einsum('bhqk,bqkhd->bqhd', p, vg)
    return o, lse.transpose(0, 2, 1)


def _merge_groups(outs, lses):
    w = jax.nn.softmax(jnp.stack(lses), axis=0)
    return jnp.sum(w[..., None] * jnp.stack(outs), axis=0)


def _dilated_prompt(qg, k, v):
    slopes = _alibi_slopes(N_HEADS_B)
    outs, lses = [], []
    for g, (win, dil) in enumerate(BRANCHES):
        o, l = _dilated_branch_prompt(qg[:, :, g], k, v, dil, win // dil, slopes)
        outs.append(o)
        lses.append(l)
    return _merge_groups(outs, lses)


def _dilated_sample(qg, kk, vv, n_past):
    slopes = _alibi_slopes(N_HEADS_B)
    outs, lses = [], []
    for g, (win, dil) in enumerate(BRANCHES):
        o, l = _dilated_branch_sample(qg[:, :, g], kk, vv, n_past, dil, win // dil, slopes)
        outs.append(o)
        lses.append(l)
    return _merge_groups(outs, lses)


def _trunk(x, sb_mix, dil_mix, g_pre_mix, g_post_mix, g_pre_mlp, g_post_mlp,
           w_qkv_a, w_o_a, w_q_b, w_o_b, g_kv, w_kv_b, w_mlp_in, w_mlp_out):
    B, T, _ = x.shape
    h = x
    k_a, v_a = [], []
    k_b = v_b = None
    for layer in range(DEPTH):
        u = _rms_norm(h, g_pre_mix[layer])
        if layer < N_A_LAYERS:
            qkv = (u @ w_qkv_a[layer]).reshape(B, T, 3, N_HEADS_A, HEAD_DIM)
            q, k, v = qkv[:, :, 0], qkv[:, :, 1], qkv[:, :, 2]
            k_a.append(k)
            v_a.append(v)
            mix = sb_mix(layer, q, k, v).reshape(B, T, D_MODEL).astype(h.dtype) @ w_o_a[layer]
        else:
            qg = (u @ w_q_b[layer - N_A_LAYERS]).reshape(B, T, N_GROUPS, N_HEADS_B, HEAD_DIM)
            mix = dil_mix(qg, k_b, v_b).reshape(B, T, D_MODEL).astype(h.dtype) @ w_o_b[layer - N_A_LAYERS]
        h = h + _rms_norm(mix, g_post_mix[layer])
        u = _rms_norm(h, g_pre_mlp[layer])
        f = jnp.square(jax.nn.relu(u @ w_mlp_in[layer])) @ w_mlp_out[layer]
        h = h + _rms_norm(f, g_post_mlp[layer])
        if layer == N_A_LAYERS - 1:
            kv = (_rms_norm(h, g_kv) @ w_kv_b).reshape(B, T, 2, N_HEADS_B, HEAD_DIM)
            k_b, v_b = kv[:, :, 0], kv[:, :, 1]
    return h, jnp.stack(k_a), jnp.stack(v_a), k_b, v_b


def setup_inputs(seed: int = 0) -> dict:
    key = jax.random.key(seed)
    ks = jax.random.split(key, 21)
    n_pages = PAST_LEN // PAGE_SIZE
    n_phys = (DEC_BATCH * n_pages * 5) // 4
    win_buf = min(WINDOW_MAX, PAST_LEN)
    f32 = jnp.float32

    def nrm(k, shape, scale=1.0):
        return jax.random.normal(k, shape, f32) * scale

    def gain(k, shape):
        return 1.0 + 0.05 * jax.random.normal(k, shape, f32)

    page_table = jax.random.permutation(ks[6], n_phys)[:DEC_BATCH * n_pages]
    page_table = page_table.reshape(DEC_BATCH, n_pages).astype(jnp.int32)
    return {
        'x_prompt': nrm(ks[0], (BATCH, SEQ, D_MODEL)),
        'x_sample': nrm(ks[1], (DEC_BATCH, DEC_SEQ, D_MODEL)),
        'cache_k_a': nrm(ks[2], (N_A_LAYERS, n_phys, PAGE_SIZE, N_HEADS_A, HEAD_DIM)),
        'cache_v_a': nrm(ks[3], (N_A_LAYERS, n_phys, PAGE_SIZE, N_HEADS_A, HEAD_DIM)),
        'cache_k_b': nrm(ks[4], (DEC_BATCH, win_buf, N_HEADS_B, HEAD_DIM)),
        'cache_v_b': nrm(ks[5], (DEC_BATCH, win_buf, N_HEADS_B, HEAD_DIM)),
        'page_table': page_table,
        'g_pre_mix': gain(ks[7], (DEPTH, D_MODEL)),
        'g_post_mix': gain(ks[8], (DEPTH, D_MODEL)),
        'g_pre_mlp': gain(ks[9], (DEPTH, D_MODEL)),
        'g_post_mlp': gain(ks[10], (DEPTH, D_MODEL)),
        'w_qkv_a': nrm(ks[11], (N_A_LAYERS, D_MODEL, 3 * N_HEADS_A * HEAD_DIM), D_MODEL ** -0.5),
        'w_o_a': nrm(ks[12], (N_A_LAYERS, N_HEADS_A * HEAD_DIM, D_MODEL), D_MODEL ** -0.5),
        'b_sb': SB_BIAS_INIT + 0.5 * jax.random.normal(ks[19], (N_A_LAYERS, N_HEADS_A), f32),
        'w_q_b': nrm(ks[13], (N_B_LAYERS, D_MODEL, N_GROUPS * N_HEADS_B * HEAD_DIM), D_MODEL ** -0.5),
        'w_o_b': nrm(ks[14], (N_B_LAYERS, N_HEADS_B * HEAD_DIM, D_MODEL), D_MODEL ** -0.5),
        'g_kv': gain(ks[15], (D_MODEL,)),
        'w_kv_b': nrm(ks[16], (D_MODEL, 2 * N_HEADS_B * HEAD_DIM), D_MODEL ** -0.5),
        'w_mlp_in': nrm(ks[17], (DEPTH, D_MODEL, D_FF), D_MODEL ** -0.5),
        'w_mlp_out': nrm(ks[18], (DEPTH, D_FF, D_MODEL), D_FF ** -0.5),
    }


def reference(x_prompt, x_sample, cache_k_a, cache_v_a, cache_k_b, cache_v_b, page_table,
              g_pre_mix, g_post_mix, g_pre_mlp, g_post_mlp, w_qkv_a, w_o_a, b_sb, w_q_b, w_o_b,
              g_kv, w_kv_b, w_mlp_in, w_mlp_out):
    def sb_prompt(layer, q, k, v):
        return _stick_breaking_prompt(q, k, v, b_sb[layer])

    y_prompt, k_a_prompt, v_a_prompt, k_b_p, v_b_p = _trunk(
        x_prompt, sb_prompt, _dilated_prompt, g_pre_mix, g_post_mix, g_pre_mlp, g_post_mlp,
        w_qkv_a, w_o_a, w_q_b, w_o_b, g_kv, w_kv_b, w_mlp_in, w_mlp_out)

    dec_b = x_sample.shape[0]
    n_ctx = page_table.shape[1] * cache_k_a.shape[2]
    n_buf = cache_k_b.shape[1]

    def sb_sample(layer, q, k, v):
        k_past = cache_k_a[layer][page_table].reshape(dec_b, n_ctx, N_HEADS_A, HEAD_DIM)
        v_past = cache_v_a[layer][page_table].reshape(dec_b, n_ctx, N_HEADS_A, HEAD_DIM)
        return _stick_breaking_sample(q, k, v, k_past, v_past, b_sb[layer])

    def dil_sample(qg, k, v):
        kk = jnp.concatenate([cache_k_b, k], axis=1)
        vv = jnp.concatenate([cache_v_b, v], axis=1)
        return _dilated_sample(qg, kk, vv, n_buf)

    y_sample, k_a_sample, v_a_sample, k_b_s, v_b_s = _trunk(
        x_sample, sb_sample, dil_sample, g_pre_mix, g_post_mix, g_pre_mlp, g_post_mlp,
        w_qkv_a, w_o_a, w_q_b, w_o_b, g_kv, w_kv_b, w_mlp_in, w_mlp_out)

    seq = x_prompt.shape[1]
    keep_p = min(WINDOW_MAX, seq)
    k_b_prompt = k_b_p[:, seq - keep_p:]
    v_b_prompt = v_b_p[:, seq - keep_p:]
    t_new = x_sample.shape[1]
    k_b_sample = jnp.concatenate([cache_k_b, k_b_s], axis=1)[:, t_new:]
    v_b_sample = jnp.concatenate([cache_v_b, v_b_s], axis=1)[:, t_new:]
    return (y_prompt, y_sample, k_a_prompt, v_a_prompt, k_a_sample, v_a_sample,
            k_b_prompt, v_b_prompt, k_b_sample, v_b_sample)
```

```python
import functools

import jax
import jax.numpy as jnp
from jax import lax
from jax.experimental import pallas as pl
from jax.experimental.pallas import tpu as pltpu

F32 = jnp.float32
BF16 = jnp.bfloat16

HEAD_DIM = 64
N_HEADS = 16
BRANCHES = ((128, 1), (512, 4), (2048, 16))
BLOCK = 128
ATTN_SCALE = HEAD_DIM ** -0.5
RMS_EPS = 1e-6
VMEM_LIMIT = 56 * 1024 * 1024


def _params(*sem):
    return pltpu.CompilerParams(dimension_semantics=sem, vmem_limit_bytes=VMEM_LIMIT)


def _rms(x, g):
    return x * lax.rsqrt(jnp.mean(x * x, axis=-1, keepdims=True) + RMS_EPS) * g


def _dot(a, b):
    return jnp.dot(a, b, preferred_element_type=F32)


def _dot_nt(a, b):
    return lax.dot_general(a, b, (((1,), (1,)), ((), ())), preferred_element_type=F32)


def _norm_matmul_kernel(x_ref, g_ref, w_ref, *out_refs):
    u = _rms(x_ref[...], g_ref[...]).astype(BF16)
    width = out_refs[0].shape[-1]
    for i, o_ref in enumerate(out_refs):
        o_ref[...] = _dot(u, w_ref[:, i * width:(i + 1) * width])


def _norm_matmul(x, g, w, n_out, tm):
    m, d = x.shape
    width = w.shape[1] // n_out
    return pl.pallas_call(
        _norm_matmul_kernel,
        grid=(m // tm,),
        in_specs=[
            pl.BlockSpec((tm, d), lambda i: (i, 0)),
            pl.BlockSpec((1, d), lambda i: (0, 0)),
            pl.BlockSpec(w.shape, lambda i: (0, 0)),
        ],
        out_specs=[pl.BlockSpec((tm, width), lambda i: (i, 0))] * n_out,
        out_shape=[jax.ShapeDtypeStruct((m, width), F32)] * n_out,
        compiler_params=_params("parallel"),
        name="norm_matmul",
    )(x, g.reshape(1, d), w)


def _matmul_norm_res_kernel(a_ref, w_ref, g_ref, h_ref, o_ref):
    y = _dot(a_ref[...].astype(BF16), w_ref[...])
    o_ref[...] = h_ref[...] + _rms(y, g_ref[...])


def _matmul_norm_res(a, w, g, h, tm):
    m, k = a.shape
    d = w.shape[1]
    return pl.pallas_call(
        _matmul_norm_res_kernel,
        grid=(m // tm,),
        in_specs=[
            pl.BlockSpec((tm, k), lambda i: (i, 0)),
            pl.BlockSpec(w.shape, lambda i: (0, 0)),
            pl.BlockSpec((1, d), lambda i: (0, 0)),
            pl.BlockSpec((tm, d), lambda i: (i, 0)),
        ],
        out_specs=pl.BlockSpec((tm, d), lambda i: (i, 0)),
        out_shape=jax.ShapeDtypeStruct((m, d), F32),
        compiler_params=_params("parallel"),
        name="matmul_norm_res",
    )(a, w, g.reshape(1, d), h)


def _merge_matmul_norm_res_kernel(o0, o1, o2, l0, l1, l2, w_ref, g_ref, h_ref, out_ref):
    a0, a1, a2 = l0[...], l1[...], l2[...]
    mx = jnp.maximum(jnp.maximum(a0, a1), a2)
    e0, e1, e2 = jnp.exp(a0 - mx), jnp.exp(a1 - mx), jnp.exp(a2 - mx)
    den = e0 + e1 + e2
    merged = (e0 / den) * o0[...] + (e1 / den) * o1[...] + (e2 / den) * o2[...]
    y = _dot(merged.astype(BF16), w_ref[...])
    out_ref[...] = h_ref[...] + _rms(y, g_ref[...])


def _merge_matmul_norm_res(outs, lses, w, g, h, tm):
    m, d = h.shape
    tok = pl.BlockSpec((tm, d), lambda i: (i, 0))
    return pl.pallas_call(
        _merge_matmul_norm_res_kernel,
        grid=(m // tm,),
        in_specs=[tok] * 6 + [
            pl.BlockSpec(w.shape, lambda i: (0, 0)),
            pl.BlockSpec((1, d), lambda i: (0, 0)),
            tok,
        ],
        out_specs=tok,
        out_shape=jax.ShapeDtypeStruct((m, d), F32),
        compiler_params=_params("parallel"),
        name="merge_matmul_norm_res",
    )(*outs, *lses, w, g.reshape(1, d), h)


def _mlp_kernel(h_ref, g1_ref, win_ref, wout_ref, g2_ref, o_ref, *, chunk):
    h = h_ref[...]
    u = _rms(h, g1_ref[...]).astype(BF16)
    acc = jnp.zeros(h.shape, F32)
    for c in range(win_ref.shape[1] // chunk):
        a = _dot(u, win_ref[:, c * chunk:(c + 1) * chunk])
        a = jnp.square(jnp.maximum(a, 0.0)).astype(BF16)
        acc = acc + _dot(a, wout_ref[c * chunk:(c + 1) * chunk, :])
    o_ref[...] = h + _rms(acc, g2_ref[...])


def _mlp(h, g1, w_in, w_out, g2, tm):
    m, d = h.shape
    return pl.pallas_call(
        functools.partial(_mlp_kernel, chunk=1024),
        grid=(m // tm,),
        in_specs=[
            pl.BlockSpec((tm, d), lambda i: (i, 0)),
            pl.BlockSpec((1, d), lambda i: (0, 0)),
            pl.BlockSpec(w_in.shape, lambda i: (0, 0)),
            pl.BlockSpec(w_out.shape, lambda i: (0, 0)),
            pl.BlockSpec((1, d), lambda i: (0, 0)),
        ],
        out_specs=pl.BlockSpec((tm, d), lambda i: (i, 0)),
        out_shape=jax.ShapeDtypeStruct((m, d), F32),
        compiler_params=_params("parallel"),
        name="mlp",
    )(h, g1.reshape(1, d), w_in, w_out, g2.reshape(1, d))


def _split_bf16(x):
    hi = x.astype(BF16)
    return hi, (x - hi.astype(F32)).astype(BF16)


def _sb_prompt_kernel(bias_ref, q_ref, k_ref, v_ref, o_ref, *, tq):
    hp = pl.program_id(1)
    i = pl.program_id(2)
    row = lax.broadcasted_iota(jnp.int32, (tq, tq), 0)
    col = lax.broadcasted_iota(jnp.int32, (tq, tq), 1)
    later = (row > col).astype(BF16)
    outs = []
    for hh in range(2):
        sl = slice(hh * HEAD_DIM, (hh + 1) * HEAD_DIM)
        bias = bias_ref[0, 2 * hp + hh]
        qh = (q_ref[:, sl] * ATTN_SCALE).astype(BF16)

        def body(jj, carry, sl=sl, bias=bias, qh=qh):
            rem_tail, acc = carry
            j = i - jj
            start = pl.multiple_of(j * tq, tq)
            kh = k_ref[pl.ds(start, tq), sl].astype(BF16)
            vh = v_ref[pl.ds(start, tq), sl].astype(BF16)
            z = _dot_nt(qh, kh) + bias
            mask = (j * tq + col) < (i * tq + row)
            t = jnp.log1p(jnp.exp(-jnp.abs(z)))
            ell = jnp.where(mask, -jnp.maximum(z, 0.0) - t, 0.0)
            lsig = jnp.minimum(z, 0.0) - t
            ell_hi, ell_lo = _split_bf16(ell)
            rem = _dot(ell_hi, later) + _dot(ell_lo, later) + rem_tail
            a = jnp.where(mask, jnp.exp(lsig + rem), 0.0)
            acc = acc + _dot(a.astype(BF16), vh)
            rem_tail = rem_tail + jnp.sum(ell, axis=-1, keepdims=True)
            return rem_tail, acc

        _, acc = lax.fori_loop(
            0, i + 1, body, (jnp.zeros((tq, 1), F32), jnp.zeros((tq, HEAD_DIM), F32)))
        outs.append(acc)
    o_ref[...] = jnp.concatenate(outs, axis=-1)


def _sb_prompt(q, k, v, bias, batch, seq):
    d = q.shape[1]
    tq = BLOCK
    q3, k3, v3 = (a.reshape(batch, seq, d) for a in (q, k, v))
    lanes = 2 * HEAD_DIM
    out = pl.pallas_call(
        functools.partial(_sb_prompt_kernel, tq=tq),
        grid=(batch, d // lanes, seq // tq),
        in_specs=[
            pl.BlockSpec(memory_space=pltpu.SMEM),
            pl.BlockSpec((None, tq, lanes), lambda b, h, i: (b, i, h)),
            pl.BlockSpec((None, seq, lanes), lambda b, h, i: (b, 0, h)),
            pl.BlockSpec((None, seq, lanes), lambda b, h, i: (b, 0, h)),
        ],
        out_specs=pl.BlockSpec((None, tq, lanes), lambda b, h, i: (b, i, h)),
        out_shape=jax.ShapeDtypeStruct((batch, seq, d), F32),
        compiler_params=_params("parallel", "parallel", "arbitrary"),
        name="sb_prompt",
    )(bias.reshape(1, -1), q3, k3, v3)
    return out.reshape(batch * seq, d)


ROWS = 16


def _head_rows(ref, h, n_keys, offset=0, stride=N_HEADS):
    return ref[pl.ds(offset + h, n_keys, stride=stride), :]


def _sb_sample_kernel(pt_ref, wq_ref, bias_ref, knew_ref, vnew_ref, *rest, pp, t_new):
    k_refs, v_refs = rest[:pp], rest[pp:2 * pp]
    o_ref, tail_ref, acc_ref = rest[2 * pp:]
    j = pl.program_id(1)
    nk = BLOCK
    row = lax.broadcasted_iota(jnp.int32, (nk, nk), 0)
    col = lax.broadcasted_iota(jnp.int32, (nk, nk), 1)
    later = (col > row).astype(BF16)

    def block(k_ref, v_ref, mask):
        def qk(h, acc):
            return acc + _dot(_head_rows(k_ref, h, nk).astype(BF16), wq_ref[h])
        z = lax.fori_loop(0, N_HEADS, qk, jnp.zeros((nk, nk), F32)) + bias_ref[...]
        t = jnp.log1p(jnp.exp(-jnp.abs(z)))
        ell = -jnp.maximum(z, 0.0) - t
        lsig = jnp.minimum(z, 0.0) - t
        if mask is not None:
            ell = jnp.where(mask, ell, 0.0)
        ell_hi, ell_lo = _split_bf16(ell)
        rem = _dot(later, ell_hi) + _dot(later, ell_lo) + tail_ref[...]
        a = jnp.exp(lsig + rem)
        if mask is not None:
            a = jnp.where(mask, a, 0.0)
        a_t = a.T.astype(BF16)

        for g in range(nk // 2 // ROWS):
            lhs = a_t[g * ROWS:(g + 1) * ROWS, :]
            for hh in range(ROWS // t_new):
                h = g * (ROWS // t_new) + hh
                acc_ref[h] += _dot(lhs, _head_rows(v_ref, h, nk).astype(BF16))
        tail_ref[...] += jnp.sum(ell, axis=0, keepdims=True)

    @pl.when(j == 0)
    def _():
        tail_ref[...] = jnp.zeros_like(tail_ref)
        acc_ref[...] = jnp.zeros_like(acc_ref)
        block(knew_ref, vnew_ref, row < (col % t_new))

    for p in range(pp):
        block(k_refs[p], v_refs[p], None)

    @pl.when(j == pl.num_programs(1) - 1)
    def _():
        o_ref[...] = acc_ref[...]


def _lane_queries(q, t_new):
    b = q.shape[0]
    eye = jnp.eye(N_HEADS, dtype=q.dtype)
    w = jnp.einsum('bthd,hg->bhdgt', q, eye).reshape(b, N_HEADS, HEAD_DIM, N_HEADS * t_new)
    return jnp.pad(w, ((0, 0), (0, 0), (0, 0), (0, BLOCK - N_HEADS * t_new))).astype(BF16)


def _lane_heads(x, t_new):
    return jnp.pad(jnp.repeat(x.astype(F32), t_new), (0, BLOCK - N_HEADS * t_new)).reshape(1, BLOCK)


def _pad_new_keys(x, batch, t_new):
    x = x.reshape(batch, t_new, N_HEADS, HEAD_DIM)
    x = jnp.pad(x, ((0, 0), (0, BLOCK - t_new), (0, 0), (0, 0)))
    return x.reshape(batch, BLOCK * N_HEADS, HEAD_DIM)


def _rows_to_tokens(o, batch, t_new):
    o = o.reshape(batch, N_HEADS // (ROWS // t_new), ROWS // t_new, ROWS // t_new, t_new, HEAD_DIM)
    o = jnp.diagonal(o, axis1=2, axis2=3)
    o = jnp.moveaxis(o, -1, 2)
    o = o.reshape(batch, N_HEADS, t_new, HEAD_DIM).transpose(0, 2, 1, 3)
    return o.reshape(batch * t_new, N_HEADS * HEAD_DIM)


def _sb_sample(q, k_new, v_new, cache_k, cache_v, page_table, bias, layer, batch, t_new):
    n_layers, n_phys, page, _, _ = cache_k.shape
    n_pages = page_table.shape[1]
    pp = 4
    rows = page * N_HEADS
    ck = cache_k.reshape(n_layers * n_phys, rows, HEAD_DIM)
    cv = cache_v.reshape(n_layers * n_phys, rows, HEAD_DIM)
    wq = _lane_queries(q.reshape(batch, t_new, N_HEADS, HEAD_DIM) * ATTN_SCALE, t_new)

    def page_spec(p):
        def index(b, j, pt):
            return (layer * n_phys + pt[b, n_pages - 1 - (j * pp + p)], 0, 0)
        return pl.BlockSpec((None, rows, HEAD_DIM), index)

    slab = pl.BlockSpec((None, rows, HEAD_DIM), lambda b, j, pt: (b, 0, 0))
    out = pl.pallas_call(
        functools.partial(_sb_sample_kernel, pp=pp, t_new=t_new),
        grid_spec=pltpu.PrefetchScalarGridSpec(
            num_scalar_prefetch=1,
            grid=(batch, n_pages // pp),
            in_specs=[
                pl.BlockSpec((None, N_HEADS, HEAD_DIM, BLOCK), lambda b, j, pt: (b, 0, 0, 0)),
                pl.BlockSpec((1, BLOCK), lambda b, j, pt: (0, 0)),
                slab, slab,
            ] + [page_spec(p) for p in range(pp)] * 2,
            out_specs=pl.BlockSpec((None, N_HEADS, ROWS, HEAD_DIM), lambda b, j, pt: (b, 0, 0, 0)),
            scratch_shapes=[
                pltpu.VMEM((1, BLOCK), F32),
                pltpu.VMEM((N_HEADS, ROWS, HEAD_DIM), F32),
            ],
        ),
        out_shape=jax.ShapeDtypeStruct((batch, N_HEADS, ROWS, HEAD_DIM), F32),
        compiler_params=_params("parallel", "arbitrary"),
        name="sb_sample",
    )(page_table, wq, _lane_heads(bias, t_new), _pad_new_keys(k_new, batch, t_new),
      _pad_new_keys(v_new, batch, t_new), *([ck] * pp), *([cv] * pp))
    return _rows_to_tokens(out, batch, t_new)


def _alibi_slopes(n_heads):
    return 2.0 ** (-8.0 * (jnp.arange(n_heads, dtype=F32) + 1.0) / n_heads)


def _dil_prompt_kernel(slope_ref, q_ref, kc_ref, kp_ref, vc_ref, vp_ref, o_ref, l_ref, *, dil):
    n = pl.program_id(2)
    row = lax.broadcasted_iota(jnp.int32, (BLOCK, BLOCK), 0)
    col = lax.broadcasted_iota(jnp.int32, (BLOCK, BLOCK), 1)
    delta_c = row - col
    delta_p = delta_c + BLOCK
    valid_c = delta_c >= 0
    valid_p = jnp.logical_and(delta_p <= BLOCK, n > 0)
    dist_c = (delta_c * dil).astype(F32)
    dist_p = (delta_p * dil).astype(F32)
    for h in range(N_HEADS):
        sl = slice(h * HEAD_DIM, (h + 1) * HEAD_DIM)
        slope = slope_ref[0, h]
        qh = (q_ref[:, sl] * ATTN_SCALE).astype(BF16)
        s_c = _dot_nt(qh, kc_ref[:, sl].astype(BF16)) - slope * dist_c
        s_p = _dot_nt(qh, kp_ref[:, sl].astype(BF16)) - slope * dist_p
        s_c = jnp.where(valid_c, s_c, -jnp.inf)
        s_p = jnp.where(valid_p, s_p, -jnp.inf)
        m = jnp.maximum(jnp.max(s_c, axis=-1, keepdims=True), jnp.max(s_p, axis=-1, keepdims=True))
        p_c = jnp.exp(s_c - m)
        p_p = jnp.exp(s_p - m)
        l = jnp.sum(p_c, axis=-1, keepdims=True) + jnp.sum(p_p, axis=-1, keepdims=True)
        o = _dot(p_c.astype(BF16), vc_ref[:, sl].astype(BF16)) + _dot(p_p.astype(BF16), vp_ref[:, sl].astype(BF16))
        o_ref[:, sl] = o / l
        l_ref[:, sl] = jnp.broadcast_to(m + jnp.log(l), (BLOCK, HEAD_DIM))


def _dil_prompt_branch(q, k, v, slopes, dil, batch, seq):
    d = q.shape[1]
    length = seq // dil
    nb = length // BLOCK

    def view(a):
        return a.reshape(batch, length, dil * d)

    cur = pl.BlockSpec((None, BLOCK, d), lambda b, r, n: (b, n, r))
    prev = pl.BlockSpec((None, BLOCK, d), lambda b, r, n: (b, jnp.maximum(n - 1, 0), r))
    o, lse = pl.pallas_call(
        functools.partial(_dil_prompt_kernel, dil=dil),
        grid=(batch, dil, nb),
        in_specs=[pl.BlockSpec(memory_space=pltpu.SMEM), cur, cur, prev, cur, prev],
        out_specs=[cur, cur],
        out_shape=[jax.ShapeDtypeStruct((batch, length, dil * d), F32)] * 2,
        compiler_params=_params("parallel", "parallel", "arbitrary"),
        name=f"dil_prompt_{dil}",
    )(slopes.reshape(1, -1), view(q), view(k), view(k), view(v), view(v))
    return o.reshape(batch * seq, d), lse.reshape(batch * seq, d)


def _dil_sample_kernel(wq_ref, slope_ref, knew_ref, vnew_ref, k1_ref, v1_ref, k2_ref, v2_ref,
                       k3_ref, v3_ref, o_ref, kk_ref, vv_ref, acc_ref, *, t_new):
    nk = BLOCK
    row = lax.broadcasted_iota(jnp.int32, (nk, nk), 0)
    col = lax.broadcasted_iota(jnp.int32, (nk, nk), 1)
    lane_t = col % t_new
    steps_back = (nk - row).astype(F32)
    self_mask = row == lane_t
    sub = lax.broadcasted_iota(jnp.int32, (ROWS, HEAD_DIM), 0)
    heads_per_group = ROWS // t_new

    tail = nk * N_HEADS
    kk_ref[0:tail, :] = k1_ref[...]
    vv_ref[0:tail, :] = v1_ref[...]
    kk_ref[tail:tail + ROWS * N_HEADS // 2, :] = knew_ref[0:ROWS * N_HEADS // 2, :]
    vv_ref[tail:tail + ROWS * N_HEADS // 2, :] = vnew_ref[0:ROWS * N_HEADS // 2, :]

    sources = (
        (kk_ref, vv_ref, lambda ref, t, h: _head_rows(ref, h, nk, t * N_HEADS)),
        (k2_ref, v2_ref, lambda ref, t, h: _head_rows(ref, h, nk, t * N_HEADS, N_HEADS * BRANCHES[1][1])),
        (k3_ref, v3_ref, lambda ref, t, h: ref[:, t * N_HEADS + h, :]),
    )
    lses, inv_ls = [], []
    for g, (kref, vref, load) in enumerate(sources):
        dil = BRANCHES[g][1]
        s = jnp.zeros((nk, nk), F32)
        for t in range(t_new):
            def qk(h, acc, kref=kref, t=t, g=g, load=load):
                return acc + _dot(load(kref, t, h).astype(BF16), wq_ref[g, h])
            s_t = lax.fori_loop(0, N_HEADS, qk, jnp.zeros((nk, nk), F32))
            s = s + jnp.where(lane_t == t, s_t, 0.0)
        s = s - slope_ref[...] * (steps_back * float(dil))

        def qk_self(h, acc, g=g):
            return acc + _dot(_head_rows(knew_ref, h, nk).astype(BF16), wq_ref[g, h])
        s_self = lax.fori_loop(0, N_HEADS, qk_self, jnp.zeros((nk, nk), F32))
        s_self = jnp.where(self_mask, s_self, -jnp.inf)

        m = jnp.maximum(jnp.max(s, axis=0, keepdims=True), jnp.max(s_self, axis=0, keepdims=True))
        p = jnp.exp(s - m)
        p_self = jnp.exp(s_self - m)
        l = jnp.sum(p, axis=0, keepdims=True) + jnp.sum(p_self, axis=0, keepdims=True)
        lses.append(m + jnp.log(l))
        inv_ls.append(1.0 / l)
        p_t = p.T
        p_self_t = p_self.T.astype(BF16)
        for grp in range(nk // 2 // ROWS):
            lhs_self = p_self_t[grp * ROWS:(grp + 1) * ROWS, :]
            lhs = p_t[grp * ROWS:(grp + 1) * ROWS, :].astype(BF16)
            for hh in range(heads_per_group):
                h = grp * heads_per_group + hh
                o = _dot(lhs_self, _head_rows(vnew_ref, h, nk).astype(BF16))
                for t in range(t_new):
                    vh = load(vref, t, h).astype(BF16)
                    o = o + jnp.where(sub == hh * t_new + t, _dot(lhs, vh), 0.0)
                acc_ref[g, h] = o

    mx = jnp.maximum(jnp.maximum(lses[0], lses[1]), lses[2])
    es = [jnp.exp(x - mx) for x in lses]
    den = es[0] + es[1] + es[2]
    coefs = [jnp.broadcast_to(e / den * inv, (nk, nk)).T for e, inv in zip(es, inv_ls)]
    for h in range(N_HEADS):
        grp = h // heads_per_group
        o = jnp.zeros((ROWS, HEAD_DIM), F32)
        for g in range(len(BRANCHES)):
            o = o + coefs[g][grp * ROWS:(grp + 1) * ROWS, 0:HEAD_DIM] * acc_ref[g, h]
        o_ref[h] = o


def _dil_sample(qg, k_new, v_new, cache_k, cache_v, slopes, batch, t_new):
    n_buf = cache_k.shape[1]
    assert n_buf == BRANCHES[2][0] and t_new == BRANCHES[1][1] and BRANCHES[2][1] == N_HEADS
    rows = BLOCK * N_HEADS
    wq = jnp.stack([
        _lane_queries(q.reshape(batch, t_new, N_HEADS, HEAD_DIM) * ATTN_SCALE, t_new) for q in qg], axis=1)
    flat_k = cache_k.reshape(batch, n_buf * N_HEADS, HEAD_DIM)
    flat_v = cache_v.reshape(batch, n_buf * N_HEADS, HEAD_DIM)
    cls_k = cache_k.reshape(batch, n_buf // N_HEADS, N_HEADS * N_HEADS, HEAD_DIM)
    cls_v = cache_v.reshape(batch, n_buf // N_HEADS, N_HEADS * N_HEADS, HEAD_DIM)
    n1 = n_buf // BRANCHES[0][0]
    n2 = n_buf // BRANCHES[1][0]
    slab = pl.BlockSpec((None, rows, HEAD_DIM), lambda b: (b, 0, 0))
    win1 = pl.BlockSpec((None, rows, HEAD_DIM), lambda b: (b, n1 - 1, 0))
    win2 = pl.BlockSpec((None, rows * t_new, HEAD_DIM), lambda b: (b, n2 - 1, 0))
    win3 = pl.BlockSpec((None, BLOCK, N_HEADS * t_new, HEAD_DIM), lambda b: (b, 0, 0, 0))
    out = pl.pallas_call(
        functools.partial(_dil_sample_kernel, t_new=t_new),
        grid=(batch,),
        in_specs=[
            pl.BlockSpec((None, len(BRANCHES), N_HEADS, HEAD_DIM, BLOCK), lambda b: (b, 0, 0, 0, 0)),
            pl.BlockSpec((1, BLOCK), lambda b: (0, 0)),
            slab, slab, win1, win1, win2, win2, win3, win3,
        ],
        out_specs=pl.BlockSpec((None, N_HEADS, ROWS, HEAD_DIM), lambda b: (b, 0, 0, 0)),
        out_shape=jax.ShapeDtypeStruct((batch, N_HEADS, ROWS, HEAD_DIM), F32),
        scratch_shapes=[
            pltpu.VMEM((rows + ROWS * N_HEADS // 2, HEAD_DIM), F32),
            pltpu.VMEM((rows + ROWS * N_HEADS // 2, HEAD_DIM), F32),
            pltpu.VMEM((len(BRANCHES), N_HEADS, ROWS, HEAD_DIM), F32),
        ],
        compiler_params=_params("parallel"),
        name="dil_sample",
    )(wq, _lane_heads(slopes, t_new), _pad_new_keys(k_new, batch, t_new), _pad_new_keys(v_new, batch, t_new),
      flat_k, flat_v, flat_k, flat_v, cls_k, cls_v)
    return _rows_to_tokens(out, batch, t_new)


def _shift_kernel(ck_ref, ck_next_ref, k_new_ref, cv_ref, cv_next_ref, v_new_ref, ok_ref, ov_ref, *, t_new):
    last = pl.program_id(1) == pl.num_programs(1) - 1
    rows = ok_ref.shape[0]
    for c_ref, nxt_ref, new_ref, o_ref in ((ck_ref, ck_next_ref, k_new_ref, ok_ref),
                                           (cv_ref, cv_next_ref, v_new_ref, ov_ref)):
        o_ref[0:rows - t_new] = c_ref[t_new:rows]
        o_ref[rows - t_new:rows] = jnp.where(last, new_ref[...], nxt_ref[...])


def _shift_window(cache_k, cache_v, k_new, v_new, t_new):
    batch, n_buf, nh, hd = cache_k.shape
    rows = 512
    n_chunks = n_buf // rows
    units = n_buf // t_new
    per_chunk = rows // t_new
    body = pl.BlockSpec((None, rows, nh, hd), lambda b, j: (b, j, 0, 0))
    nxt = pl.BlockSpec((None, None, t_new, nh, hd),
                       lambda b, j: (b, jnp.minimum((j + 1) * per_chunk, units - 1), 0, 0, 0))
    new = pl.BlockSpec((None, t_new, nh, hd), lambda b, j: (b, 0, 0, 0))
    ck5 = cache_k.reshape(batch, units, t_new, nh, hd)
    cv5 = cache_v.reshape(batch, units, t_new, nh, hd)
    return pl.pallas_call(
        functools.partial(_shift_kernel, t_new=t_new),
        grid=(batch, n_chunks),
        in_specs=[body, nxt, new, body, nxt, new],
        out_specs=[body, body],
        out_shape=[jax.ShapeDtypeStruct(cache_k.shape, cache_k.dtype)] * 2,
        compiler_params=_params("parallel", "arbitrary"),
        name="shift_window",
    )(cache_k, ck5, k_new, cache_v, cv5, v_new)


def kernel(x_prompt, x_sample, cache_k_a, cache_v_a, cache_k_b, cache_v_b, page_table, g_pre_mix, g_post_mix, g_pre_mlp, g_post_mlp, w_qkv_a, w_o_a, b_sb, w_q_b, w_o_b, g_kv, w_kv_b, w_mlp_in, w_mlp_out):
    batch, seq, d = x_prompt.shape
    dec_b, t_new, _ = x_sample.shape
    depth = g_pre_mix.shape[0]
    n_a = w_qkv_a.shape[0]
    assert depth == 2 and n_a == 1 and seq <= BRANCHES[2][0]
    tm_p, tm_s = 512, dec_b * t_new
    slopes = _alibi_slopes(N_HEADS)
    bf = lambda w: w.astype(BF16)

    xp = x_prompt.reshape(batch * seq, d)
    xs = x_sample.reshape(dec_b * t_new, d)

    wqkv = bf(w_qkv_a[0])
    qp, kp, vp = _norm_matmul(xp, g_pre_mix[0], wqkv, 3, tm_p)
    qs, ks, vs = _norm_matmul(xs, g_pre_mix[0], wqkv, 3, tm_s)
    ap = _sb_prompt(qp, kp, vp, b_sb[0], batch, seq)
    as_ = _sb_sample(qs, ks, vs, cache_k_a, cache_v_a, page_table, b_sb[0], 0, dec_b, t_new)
    wo = bf(w_o_a[0])
    hp = _matmul_norm_res(ap, wo, g_post_mix[0], xp, tm_p)
    hs = _matmul_norm_res(as_, wo, g_post_mix[0], xs, tm_s)
    w_in, w_out = bf(w_mlp_in[0]), bf(w_mlp_out[0])
    hp = _mlp(hp, g_pre_mlp[0], w_in, w_out, g_post_mlp[0], 256)
    hs = _mlp(hs, g_pre_mlp[0], w_in, w_out, g_post_mlp[0], tm_s)

    wkv = bf(w_kv_b)
    kbp, vbp = _norm_matmul(hp, g_kv, wkv, 2, tm_p)
    kbs, vbs = _norm_matmul(hs, g_kv, wkv, 2, tm_s)

    wq = bf(w_q_b[0])
    qgp = _norm_matmul(hp, g_pre_mix[1], wq, 3, tm_p)
    qgs = _norm_matmul(hs, g_pre_mix[1], wq, 3, tm_s)
    outs, lses = [], []
    for g, (win, dil) in enumerate(BRANCHES):
        assert win // dil == BLOCK
        o, l = _dil_prompt_branch(qgp[g], kbp, vbp, slopes, dil, batch, seq)
        outs.append(o)
        lses.append(l)
    wo = bf(w_o_b[0])
    hp = _merge_matmul_norm_res(outs, lses, wo, g_post_mix[1], hp, tm_p)
    ms = _dil_sample(qgs, kbs, vbs, cache_k_b, cache_v_b, slopes, dec_b, t_new)
    hs = _matmul_norm_res(ms, wo, g_post_mix[1], hs, tm_s)
    w_in, w_out = bf(w_mlp_in[1]), bf(w_mlp_out[1])
    hp = _mlp(hp, g_pre_mlp[1], w_in, w_out, g_post_mlp[1], 256)
    hs = _mlp(hs, g_pre_mlp[1], w_in, w_out, g_post_mlp[1], tm_s)

    heads = (N_HEADS, HEAD_DIM)
    k_b_sample, v_b_sample = _shift_window(
        cache_k_b, cache_v_b, kbs.reshape(dec_b, t_new, *heads), vbs.reshape(dec_b, t_new, *heads), t_new)
    return (
        hp.reshape(batch, seq, d),
        hs.reshape(dec_b, t_new, d),
        kp.reshape(1, batch, seq, *heads),
        vp.reshape(1, batch, seq, *heads),
        ks.reshape(1, dec_b, t_new, *heads),
        vs.reshape(1, dec_b, t_new, *heads),
        kbp.reshape(batch, seq, *heads),
        vbp.reshape(batch, seq, *heads),
        k_b_sample,
        v_b_sample,
    )
```

```python
import functools

import jax
import jax.numpy as jnp
from jax import lax
from jax.experimental import pallas as pl
from jax.experimental.pallas import tpu as pltpu

F32 = jnp.float32
BF16 = jnp.bfloat16

HEAD_DIM = 64
N_HEADS = 16
BRANCHES = ((128, 1), (512, 4), (2048, 16))
BLOCK = 128
LANES = 128
ATTN_SCALE = HEAD_DIM ** -0.5
RMS_EPS = 1e-6
VMEM_LIMIT = 56 * 1024 * 1024
OUT_ROWS = 16


def _params(*sem):
    return pltpu.CompilerParams(dimension_semantics=sem, vmem_limit_bytes=VMEM_LIMIT)


def _rms(x, g):
    return x * lax.rsqrt(jnp.mean(x * x, axis=-1, keepdims=True) + RMS_EPS) * g


def _dot(a, b):
    return jnp.dot(a, b, preferred_element_type=F32)


def _dot_nt(a, b):
    return lax.dot_general(a, b, (((1,), (1,)), ((), ())), preferred_element_type=F32)


def _split_bf16(x):
    hi = x.astype(BF16)
    return hi, (x - hi.astype(F32)).astype(BF16)


def _norm_matmul_kernel(x_ref, g_ref, w_ref, *out_refs):
    u = _rms(x_ref[...], g_ref[...]).astype(BF16)
    width = out_refs[0].shape[-1]
    for i, o_ref in enumerate(out_refs):
        o_ref[...] = _dot(u, w_ref[:, i * width:(i + 1) * width])


def _norm_matmul(x, g, w, n_out, tm):
    m, d = x.shape
    width = w.shape[1] // n_out
    return pl.pallas_call(
        _norm_matmul_kernel,
        grid=(m // tm,),
        in_specs=[
            pl.BlockSpec((tm, d), lambda i: (i, 0)),
            pl.BlockSpec((1, d), lambda i: (0, 0)),
            pl.BlockSpec(w.shape, lambda i: (0, 0)),
        ],
        out_specs=[pl.BlockSpec((tm, width), lambda i: (i, 0))] * n_out,
        out_shape=[jax.ShapeDtypeStruct((m, width), F32)] * n_out,
        compiler_params=_params("parallel"),
        name="norm_matmul",
    )(x, g.reshape(1, d), w)


def _norm_matmul_t_kernel(x_ref, g_ref, w_ref, wt_ref, *out_refs, n_row):
    u = _rms(x_ref[...], g_ref[...]).astype(BF16)
    width = x_ref.shape[-1]
    for i, o_ref in enumerate(out_refs[:n_row]):
        o_ref[...] = _dot(u, w_ref[:, i * width:(i + 1) * width])
    for i, o_ref in enumerate(out_refs[n_row:]):
        o_ref[...] = _dot_nt(wt_ref[i * width:(i + 1) * width, :], u)


def _norm_matmul_t(x, g, w, wt, batch, seq, tm):
    m, d = x.shape
    n_row, n_t = w.shape[1] // d, wt.shape[0] // d
    per_seq = seq // tm
    return pl.pallas_call(
        functools.partial(_norm_matmul_t_kernel, n_row=n_row),
        grid=(m // tm,),
        in_specs=[
            pl.BlockSpec((tm, d), lambda i: (i, 0)),
            pl.BlockSpec((1, d), lambda i: (0, 0)),
            pl.BlockSpec(w.shape, lambda i: (0, 0)),
            pl.BlockSpec(wt.shape, lambda i: (0, 0)),
        ],
        out_specs=[pl.BlockSpec((tm, d), lambda i: (i, 0))] * n_row
        + [pl.BlockSpec((None, d, tm), lambda i: (i // per_seq, 0, i % per_seq))] * n_t,
        out_shape=[jax.ShapeDtypeStruct((m, d), F32)] * n_row
        + [jax.ShapeDtypeStruct((batch, d, seq), F32)] * n_t,
        compiler_params=_params("parallel"),
        name="norm_matmul_t",
    )(x, g.reshape(1, d), w, wt)


def _matmul_norm_res_kernel(a_ref, w_ref, g_ref, h_ref, o_ref):
    y = _dot(a_ref[...].astype(BF16), w_ref[...])
    o_ref[...] = h_ref[...] + _rms(y, g_ref[...])


def _matmul_norm_res(a, w, g, h, tm):
    m, k = a.shape
    d = w.shape[1]
    return pl.pallas_call(
        _matmul_norm_res_kernel,
        grid=(m // tm,),
        in_specs=[
            pl.BlockSpec((tm, k), lambda i: (i, 0)),
            pl.BlockSpec(w.shape, lambda i: (0, 0)),
            pl.BlockSpec((1, d), lambda i: (0, 0)),
            pl.BlockSpec((tm, d), lambda i: (i, 0)),
        ],
        out_specs=pl.BlockSpec((tm, d), lambda i: (i, 0)),
        out_shape=jax.ShapeDtypeStruct((m, d), F32),
        compiler_params=_params("parallel"),
        name="matmul_norm_res",
    )(a, w, g.reshape(1, d), h)


def _merge_matmul_norm_res_kernel(o0, o1, o2, l0, l1, l2, w_ref, g_ref, h_ref, out_ref):
    a0, a1, a2 = l0[...], l1[...], l2[...]
    mx = jnp.maximum(jnp.maximum(a0, a1), a2)
    e0, e1, e2 = jnp.exp(a0 - mx), jnp.exp(a1 - mx), jnp.exp(a2 - mx)
    den = e0 + e1 + e2
    merged = (e0 / den) * o0[...] + (e1 / den) * o1[...] + (e2 / den) * o2[...]
    y = _dot(merged.astype(BF16), w_ref[...])
    out_ref[...] = h_ref[...] + _rms(y, g_ref[...])


def _merge_matmul_norm_res(outs, lses, w, g, h, tm):
    m, d = h.shape
    tok = pl.BlockSpec((tm, d), lambda i: (i, 0))
    return pl.pallas_call(
        _merge_matmul_norm_res_kernel,
        grid=(m // tm,),
        in_specs=[tok] * 6 + [
            pl.BlockSpec(w.shape, lambda i: (0, 0)),
            pl.BlockSpec((1, d), lambda i: (0, 0)),
            tok,
        ],
        out_specs=tok,
        out_shape=jax.ShapeDtypeStruct((m, d), F32),
        compiler_params=_params("parallel"),
        name="merge_matmul_norm_res",
    )(*outs, *lses, w, g.reshape(1, d), h)


def _mlp_kernel(h_ref, g1_ref, win_ref, wout_ref, g2_ref, o_ref, *, chunk):
    h = h_ref[...]
    u = _rms(h, g1_ref[...]).astype(BF16)
    acc = jnp.zeros(h.shape, F32)
    for c in range(win_ref.shape[1] // chunk):
        a = _dot(u, win_ref[:, c * chunk:(c + 1) * chunk])
        a = jnp.square(jnp.maximum(a, 0.0)).astype(BF16)
        acc = acc + _dot(a, wout_ref[c * chunk:(c + 1) * chunk, :])
    o_ref[...] = h + _rms(acc, g2_ref[...])


def _mlp(h, g1, w_in, w_out, g2, tm):
    m, d = h.shape
    return pl.pallas_call(
        functools.partial(_mlp_kernel, chunk=1024),
        grid=(m // tm,),
        in_specs=[
            pl.BlockSpec((tm, d), lambda i: (i, 0)),
            pl.BlockSpec((1, d), lambda i: (0, 0)),
            pl.BlockSpec(w_in.shape, lambda i: (0, 0)),
            pl.BlockSpec(w_out.shape, lambda i: (0, 0)),
            pl.BlockSpec((1, d), lambda i: (0, 0)),
        ],
        out_specs=pl.BlockSpec((tm, d), lambda i: (i, 0)),
        out_shape=jax.ShapeDtypeStruct((m, d), F32),
        compiler_params=_params("parallel"),
        name="mlp",
    )(h, g1.reshape(1, d), w_in, w_out, g2.reshape(1, d))


def _sb_tile(z, later, tail, mask):
    t = jnp.log1p(jnp.exp(-jnp.abs(z)))
    ell = -jnp.maximum(z, 0.0) - t
    lsig = jnp.minimum(z, 0.0) - t
    if mask is not None:
        ell = jnp.where(mask, ell, 0.0)
    ell_hi, ell_lo = _split_bf16(ell)
    rem = _dot(ell_hi, later) + _dot(ell_lo, later) + tail
    a = jnp.exp(lsig + rem)
    if mask is not None:
        a = jnp.where(mask, a, 0.0)
    return a, jnp.sum(ell, axis=-1, keepdims=True)


def _later_matrix(n):
    row = lax.broadcasted_iota(jnp.int32, (n, n), 0)
    col = lax.broadcasted_iota(jnp.int32, (n, n), 1)
    return (row > col).astype(BF16)


def _sb_prompt_kernel(bias_ref, later_ref, q_ref, kt_ref, vt_ref, o_ref, *, tq):
    hp = pl.program_id(1)
    i = pl.program_id(2)
    row = lax.broadcasted_iota(jnp.int32, (tq, tq), 0)
    col = lax.broadcasted_iota(jnp.int32, (tq, tq), 1)
    causal = col < row
    later = later_ref[...]
    heads = []
    for hh in range(2):
        sl = slice(hh * HEAD_DIM, (hh + 1) * HEAD_DIM)
        heads.append((sl, bias_ref[0, 2 * hp + hh], (q_ref[:, sl] * ATTN_SCALE).astype(BF16)))

    def tile(j, carry, mask):
        start = pl.multiple_of(j * tq, tq)
        out = []
        for (sl, bias, qh), (tail, acc) in zip(heads, carry):
            kth = kt_ref[sl, pl.ds(start, tq)].astype(BF16)
            vth = vt_ref[sl, pl.ds(start, tq)].astype(BF16)
            a, ell_sum = _sb_tile(_dot(qh, kth) + bias, later, tail, mask)
            out.append((tail + ell_sum, acc + _dot_nt(a.astype(BF16), vth)))
        return tuple(out)

    zero = (jnp.zeros((tq, 1), F32), jnp.zeros((tq, HEAD_DIM), F32))
    carry = tile(i, (zero, zero), causal)
    carry = lax.fori_loop(0, i, lambda jj, c: tile(i - 1 - jj, c, None), carry)
    o_ref[...] = jnp.concatenate([carry[0][1], carry[1][1]], axis=-1)


def _sb_prompt(q, kt, vt, bias, batch, seq):
    d = q.shape[1]
    tq = 256
    lanes = 2 * HEAD_DIM
    out = pl.pallas_call(
        functools.partial(_sb_prompt_kernel, tq=tq),
        grid=(batch, d // lanes, seq // tq),
        in_specs=[
            pl.BlockSpec(memory_space=pltpu.SMEM),
            pl.BlockSpec((tq, tq), lambda b, h, i: (0, 0)),
            pl.BlockSpec((None, tq, lanes), lambda b, h, i: (b, i, h)),
            pl.BlockSpec((None, lanes, seq), lambda b, h, i: (b, h, 0)),
            pl.BlockSpec((None, lanes, seq), lambda b, h, i: (b, h, 0)),
        ],
        out_specs=pl.BlockSpec((None, tq, lanes), lambda b, h, i: (b, i, h)),
        out_shape=jax.ShapeDtypeStruct((batch, seq, d), F32),
        compiler_params=_params("parallel", "parallel", "arbitrary"),
        name="sb_prompt",
    )(bias.reshape(1, -1), _later_matrix(tq), q.reshape(batch, seq, d), kt, vt)
    return out.reshape(batch * seq, d)


def _row_queries(q, batch, t_new):
    q4 = q.reshape(batch, t_new, N_HEADS, HEAD_DIM) * ATTN_SCALE
    eye = jnp.eye(N_HEADS, dtype=q.dtype)
    w = jnp.einsum('bthd,hg->bhtgd', q4, eye)
    return w.reshape(batch, N_HEADS * t_new, N_HEADS * HEAD_DIM).astype(BF16)


def _row_heads(x, t_new):
    return jnp.broadcast_to(jnp.repeat(x.astype(F32), t_new)[:, None], (N_HEADS * t_new, LANES))


def _new_keys_t(x, batch, t_new):
    xt = x.reshape(batch, t_new, N_HEADS * HEAD_DIM).transpose(0, 2, 1)
    return jnp.pad(xt, ((0, 0), (0, 0), (0, LANES - t_new)))


def _select_rows(t_new):
    r = jnp.arange(OUT_ROWS)[:, None]
    c = jnp.arange(N_HEADS * t_new)[None, :]
    return (c % t_new == r).astype(BF16)


def _diag_rows(acc, sel, t_new):
    row = lax.broadcasted_iota(jnp.int32, acc.shape, 0)
    col = lax.broadcasted_iota(jnp.int32, acc.shape, 1)
    own = jnp.where(row // t_new == col // HEAD_DIM, acc, 0.0)
    hi, lo = _split_bf16(own)
    return _dot(sel, hi) + _dot(sel, lo)


def _sb_sample_kernel(pt_ref, q_ref, bias_ref, later_ref, sel_ref, knew_ref, vnew_ref, *rest, pp, t_new):
    k_refs, v_refs = rest[:pp], rest[pp:2 * pp]
    o_ref, tail_ref, acc_ref = rest[2 * pp:]
    j = pl.program_id(1)
    q = q_ref[...]
    bias = bias_ref[:, 0:1]

    def block(kt, vt, later, mask):
        a, ell_sum = _sb_tile(_dot(q, kt) + bias, later, tail_ref[:, 0:1], mask)
        acc_ref[...] += _dot_nt(a.astype(BF16), vt)
        tail_ref[...] += ell_sum

    @pl.when(j == 0)
    def _():
        tail_ref[...] = jnp.zeros_like(tail_ref)
        acc_ref[...] = jnp.zeros_like(acc_ref)
        row = lax.broadcasted_iota(jnp.int32, (q.shape[0], LANES), 0)
        col = lax.broadcasted_iota(jnp.int32, (q.shape[0], LANES), 1)
        block(knew_ref[...].astype(BF16), vnew_ref[...].astype(BF16),
              later_ref[0:LANES, 0:LANES], col < row % t_new)

    kt = jnp.concatenate([r[...].astype(BF16) for r in k_refs], axis=1)
    vt = jnp.concatenate([r[...].astype(BF16) for r in v_refs], axis=1)
    block(kt, vt, later_ref[...], None)

    @pl.when(j == pl.num_programs(1) - 1)
    def _():
        o_ref[...] = _diag_rows(acc_ref[...], sel_ref[...], t_new)


def _sb_sample(q, k_new, v_new, cache_k, cache_v, page_table, bias, layer, batch, t_new):
    n_layers, n_phys, page, nh, hd = cache_k.shape
    d = nh * hd
    n_pages = page_table.shape[1]
    pp = 8
    rows = nh * t_new
    ck = cache_k.transpose(0, 1, 3, 4, 2).reshape(n_layers * n_phys, d, page)
    cv = cache_v.transpose(0, 1, 3, 4, 2).reshape(n_layers * n_phys, d, page)

    def page_spec(p):
        def index(b, j, pt):
            return (layer * n_phys + pt[b, n_pages - (j + 1) * pp + p], 0, 0)
        return pl.BlockSpec((None, d, page), index)

    const2 = lambda b, j, pt: (0, 0)
    per_b = lambda b, j, pt: (b, 0, 0)
    out = pl.pallas_call(
        functools.partial(_sb_sample_kernel, pp=pp, t_new=t_new),
        grid_spec=pltpu.PrefetchScalarGridSpec(
            num_scalar_prefetch=1,
            grid=(batch, n_pages // pp),
            in_specs=[
                pl.BlockSpec((None, rows, d), per_b),
                pl.BlockSpec((rows, LANES), const2),
                pl.BlockSpec((pp * page, pp * page), const2),
                pl.BlockSpec((OUT_ROWS, rows), const2),
                pl.BlockSpec((None, d, LANES), per_b),
                pl.BlockSpec((None, d, LANES), per_b),
            ] + [page_spec(p) for p in range(pp)] * 2,
            out_specs=pl.BlockSpec((None, OUT_ROWS, d), per_b),
            scratch_shapes=[pltpu.VMEM((rows, LANES), F32), pltpu.VMEM((rows, d), F32)],
        ),
        out_shape=jax.ShapeDtypeStruct((batch, OUT_ROWS, d), F32),
        compiler_params=_params("parallel", "arbitrary"),
        name="sb_sample",
    )(page_table, _row_queries(q, batch, t_new), _row_heads(bias, t_new), _later_matrix(pp * page),
      _select_rows(t_new), _new_keys_t(k_new, batch, t_new), _new_keys_t(v_new, batch, t_new),
      *([ck] * pp), *([cv] * pp))
    return out[:, :t_new].reshape(batch * t_new, d)


def _alibi_slopes(n_heads):
    return 2.0 ** (-8.0 * (jnp.arange(n_heads, dtype=F32) + 1.0) / n_heads)


def _dil_prompt_kernel(slope_ref, q_ref, kc_ref, kp_ref, vc_ref, vp_ref, o_ref, l_ref, *, dil):
    n = pl.program_id(2)
    row = lax.broadcasted_iota(jnp.int32, (BLOCK, BLOCK), 0)
    col = lax.broadcasted_iota(jnp.int32, (BLOCK, BLOCK), 1)
    delta_c = row - col
    delta_p = delta_c + BLOCK
    valid_c = delta_c >= 0
    valid_p = jnp.logical_and(delta_p <= BLOCK, n > 0)
    dist_c = (delta_c * dil).astype(F32)
    dist_p = (delta_p * dil).astype(F32)
    for h in range(N_HEADS):
        sl = slice(h * HEAD_DIM, (h + 1) * HEAD_DIM)
        slope = slope_ref[0, h]
        qh = (q_ref[:, sl] * ATTN_SCALE).astype(BF16)
        s_c = _dot_nt(qh, kc_ref[:, sl].astype(BF16)) - slope * dist_c
        s_p = _dot_nt(qh, kp_ref[:, sl].astype(BF16)) - slope * dist_p
        s_c = jnp.where(valid_c, s_c, -jnp.inf)
        s_p = jnp.where(valid_p, s_p, -jnp.inf)
        m = jnp.maximum(jnp.max(s_c, axis=-1, keepdims=True), jnp.max(s_p, axis=-1, keepdims=True))
        p_c = jnp.exp(s_c - m)
        p_p = jnp.exp(s_p - m)
        l = jnp.sum(p_c, axis=-1, keepdims=True) + jnp.sum(p_p, axis=-1, keepdims=True)
        o = _dot(p_c.astype(BF16), vc_ref[:, sl].astype(BF16)) + _dot(p_p.astype(BF16), vp_ref[:, sl].astype(BF16))
        o_ref[:, sl] = o / l
        l_ref[:, sl] = jnp.broadcast_to(m + jnp.log(l), (BLOCK, HEAD_DIM))


def _dil_prompt_branch(q, k, v, slopes, dil, batch, seq):
    d = q.shape[1]
    length = seq // dil
    nb = length // BLOCK

    def view(a):
        return a.reshape(batch, length, dil * d)

    cur = pl.BlockSpec((None, BLOCK, d), lambda b, r, n: (b, n, r))
    prev = pl.BlockSpec((None, BLOCK, d), lambda b, r, n: (b, jnp.maximum(n - 1, 0), r))
    o, lse = pl.pallas_call(
        functools.partial(_dil_prompt_kernel, dil=dil),
        grid=(batch, dil, nb),
        in_specs=[pl.BlockSpec(memory_space=pltpu.SMEM), cur, cur, prev, cur, prev],
        out_specs=[cur, cur],
        out_shape=[jax.ShapeDtypeStruct((batch, length, dil * d), F32)] * 2,
        compiler_params=_params("parallel", "parallel", "arbitrary"),
        name=f"dil_prompt_{dil}",
    )(slopes.reshape(1, -1), view(q), view(k), view(k), view(v), view(v))
    return o.reshape(batch * seq, d), lse.reshape(batch * seq, d)


def _dil_sample_kernel(q_ref, slope_ref, sel_ref, knew_ref, vnew_ref, kt_ref, vt_ref, o_ref, *, t_new):
    n_buf = kt_ref.shape[1]
    rows = q_ref.shape[1]
    slope = slope_ref[:, 0:1]
    knew = knew_ref[...].astype(BF16)
    vnew = vnew_ref[...].astype(BF16)
    outs, lses, ls = [], [], []
    for g, (win, dil) in enumerate(BRANCHES):
        q = q_ref[g]
        t_q = lax.broadcasted_iota(jnp.int32, (rows, win), 0) % t_new
        dist = win + t_q - lax.broadcasted_iota(jnp.int32, (rows, win), 1)
        valid = jnp.logical_and(jnp.bitwise_and(dist, dil - 1) == 0, dist <= win)
        s = _dot(q, kt_ref[:, n_buf - win:].astype(BF16)) - slope * dist.astype(F32)
        s = jnp.where(valid, s, -jnp.inf)
        t_n = lax.broadcasted_iota(jnp.int32, (rows, LANES), 0) % t_new
        dist_n = t_n - lax.broadcasted_iota(jnp.int32, (rows, LANES), 1)
        valid_n = jnp.logical_and(jnp.bitwise_and(dist_n, dil - 1) == 0, dist_n >= 0)
        s_n = _dot(q, knew) - slope * dist_n.astype(F32)
        s_n = jnp.where(valid_n, s_n, -jnp.inf)
        m = jnp.maximum(jnp.max(s, axis=-1, keepdims=True), jnp.max(s_n, axis=-1, keepdims=True))
        p = jnp.exp(s - m)
        p_n = jnp.exp(s_n - m)
        l = jnp.sum(p, axis=-1, keepdims=True) + jnp.sum(p_n, axis=-1, keepdims=True)
        outs.append(_dot_nt(p.astype(BF16), vt_ref[:, n_buf - win:].astype(BF16)) + _dot_nt(p_n.astype(BF16), vnew))
        lses.append(m + jnp.log(l))
        ls.append(l)
    mx = jnp.maximum(jnp.maximum(lses[0], lses[1]), lses[2])
    es = [jnp.exp(x - mx) for x in lses]
    den = es[0] + es[1] + es[2]
    merged = sum((e / den / l) * o for e, l, o in zip(es, ls, outs))
    o_ref[...] = _diag_rows(merged, sel_ref[...], t_new)


def _dil_sample(qg, k_new, v_new, cache_k, cache_v, slopes, batch, t_new):
    _, n_buf, nh, hd = cache_k.shape
    d = nh * hd
    rows = nh * t_new
    assert n_buf >= BRANCHES[-1][0] and t_new <= min(dil for _, dil in BRANCHES[1:])
    q3 = jnp.stack([_row_queries(q, batch, t_new) for q in qg], axis=1)
    kt = cache_k.transpose(0, 2, 3, 1).reshape(batch, d, n_buf)
    vt = cache_v.transpose(0, 2, 3, 1).reshape(batch, d, n_buf)
    new = pl.BlockSpec((None, d, LANES), lambda b: (b, 0, 0))
    buf = pl.BlockSpec((None, d, n_buf), lambda b: (b, 0, 0))
    out = pl.pallas_call(
        functools.partial(_dil_sample_kernel, t_new=t_new),
        grid=(batch,),
        in_specs=[
            pl.BlockSpec((None, len(BRANCHES), rows, d), lambda b: (b, 0, 0, 0)),
            pl.BlockSpec((rows, LANES), lambda b: (0, 0)),
            pl.BlockSpec((OUT_ROWS, rows), lambda b: (0, 0)),
            new, new, buf, buf,
        ],
        out_specs=pl.BlockSpec((None, OUT_ROWS, d), lambda b: (b, 0, 0)),
        out_shape=jax.ShapeDtypeStruct((batch, OUT_ROWS, d), F32),
        compiler_params=_params("parallel"),
        name="dil_sample",
    )(q3, _row_heads(slopes, t_new), _select_rows(t_new), _new_keys_t(k_new, batch, t_new),
      _new_keys_t(v_new, batch, t_new), kt, vt)
    return out[:, :t_new].reshape(batch * t_new, d)


def _shift_kernel(ck_ref, k_new_ref, cv_ref, v_new_ref, ok_ref, ov_ref, *, t_new):
    for c_ref, new_ref, o_ref in ((ck_ref, k_new_ref, ok_ref), (cv_ref, v_new_ref, ov_ref)):
        o_ref[...] = jnp.concatenate([c_ref[:, t_new:], new_ref[:, :t_new]], axis=1)


def _shift_window(cache_k, cache_v, k_new, v_new, batch, t_new):
    _, n_buf, nh, hd = cache_k.shape
    d = nh * hd
    rows = 256
    kt = cache_k.transpose(0, 2, 3, 1).reshape(batch, d, n_buf)
    vt = cache_v.transpose(0, 2, 3, 1).reshape(batch, d, n_buf)
    buf = pl.BlockSpec((None, rows, n_buf), lambda b, j: (b, j, 0))
    new = pl.BlockSpec((None, rows, LANES), lambda b, j: (b, j, 0))
    ok, ov = pl.pallas_call(
        functools.partial(_shift_kernel, t_new=t_new),
        grid=(batch, d // rows),
        in_specs=[buf, new, buf, new],
        out_specs=[buf, buf],
        out_shape=[jax.ShapeDtypeStruct((batch, d, n_buf), cache_k.dtype)] * 2,
        compiler_params=_params("parallel", "parallel"),
        name="shift_window",
    )(kt, _new_keys_t(k_new, batch, t_new), vt, _new_keys_t(v_new, batch, t_new))
    back = lambda a: a.reshape(batch, nh, hd, n_buf).transpose(0, 3, 1, 2)
    return back(ok), back(ov)


def kernel(x_prompt, x_sample, cache_k_a, cache_v_a, cache_k_b, cache_v_b, page_table, g_pre_mix, g_post_mix, g_pre_mlp, g_post_mlp, w_qkv_a, w_o_a, b_sb, w_q_b, w_o_b, g_kv, w_kv_b, w_mlp_in, w_mlp_out):
    batch, seq, d = x_prompt.shape
    dec_b, t_new, _ = x_sample.shape
    depth = g_pre_mix.shape[0]
    n_a = w_qkv_a.shape[0]
    assert depth == 2 and n_a == 1 and seq <= BRANCHES[2][0] and d == N_HEADS * HEAD_DIM
    tm_p, tm_s = 512, dec_b * t_new
    slopes = _alibi_slopes(N_HEADS)
    bf = lambda w: w.astype(BF16)
    heads = (N_HEADS, HEAD_DIM)
    from_t = lambda a: a.reshape(a.shape[0], *heads, a.shape[-1]).transpose(0, 3, 1, 2)

    xp = x_prompt.reshape(batch * seq, d)
    xs = x_sample.reshape(dec_b * t_new, d)

    wqkv = bf(w_qkv_a[0])
    qp, kpt, vpt = _norm_matmul_t(xp, g_pre_mix[0], wqkv[:, :d], wqkv[:, d:].T, batch, seq, tm_p)
    qs, ks, vs = _norm_matmul(xs, g_pre_mix[0], wqkv, 3, tm_s)
    ap = _sb_prompt(qp, kpt, vpt, b_sb[0], batch, seq)
    as_ = _sb_sample(qs, ks, vs, cache_k_a, cache_v_a, page_table, b_sb[0], 0, dec_b, t_new)
    wo = bf(w_o_a[0])
    hp = _matmul_norm_res(ap, wo, g_post_mix[0], xp, tm_p)
    hs = _matmul_norm_res(as_, wo, g_post_mix[0], xs, tm_s)
    w_in, w_out = bf(w_mlp_in[0]), bf(w_mlp_out[0])
    hp = _mlp(hp, g_pre_mlp[0], w_in, w_out, g_post_mlp[0], 256)
    hs = _mlp(hs, g_pre_mlp[0], w_in, w_out, g_post_mlp[0], tm_s)

    wkv = bf(w_kv_b)
    kbp, vbp, kbpt, vbpt = _norm_matmul_t(hp, g_kv, wkv, wkv.T, batch, seq, tm_p)
    kbs, vbs = _norm_matmul(hs, g_kv, wkv, 2, tm_s)

    wq = bf(w_q_b[0])
    qgp = _norm_matmul(hp, g_pre_mix[1], wq, 3, tm_p)
    qgs = _norm_matmul(hs, g_pre_mix[1], wq, 3, tm_s)
    outs, lses = [], []
    for g, (win, dil) in enumerate(BRANCHES):
        assert win // dil == BLOCK
        o, l = _dil_prompt_branch(qgp[g], kbp, vbp, slopes, dil, batch, seq)
        outs.append(o)
        lses.append(l)
    wo = bf(w_o_b[0])
    hp = _merge_matmul_norm_res(outs, lses, wo, g_post_mix[1], hp, tm_p)
    ms = _dil_sample(qgs, kbs, vbs, cache_k_b, cache_v_b, slopes, dec_b, t_new)
    hs = _matmul_norm_res(ms, wo, g_post_mix[1], hs, tm_s)
    w_in, w_out = bf(w_mlp_in[1]), bf(w_mlp_out[1])
    hp = _mlp(hp, g_pre_mlp[1], w_in, w_out, g_post_mlp[1], 256)
    hs = _mlp(hs, g_pre_mlp[1], w_in, w_out, g_post_mlp[1], tm_s)

    k_b_sample, v_b_sample = _shift_window(cache_k_b, cache_v_b, kbs, vbs, dec_b, t_new)
    keep = min(BRANCHES[-1][0], seq)
    return (
        hp.reshape(batch, seq, d),
        hs.reshape(dec_b, t_new, d),
        from_t(kpt)[None],
        from_t(vpt)[None],
        ks.reshape(1, dec_b, t_new, *heads),
        vs.reshape(1, dec_b, t_new, *heads),
        from_t(kbpt)[:, seq - keep:],
        from_t(vbpt)[:, seq - keep:],
        k_b_sample,
        v_b_sample,
    )
```

```python
import functools

import jax
import jax.numpy as jnp
from jax import lax
from jax.experimental import pallas as pl
from jax.experimental.pallas import tpu as pltpu

F32 = jnp.float32
BF16 = jnp.bfloat16

HEAD_DIM = 64
N_HEADS = 16
BRANCHES = ((128, 1), (512, 4), (2048, 16))
BLOCK = 128
LANES = 128
ATTN_SCALE = HEAD_DIM ** -0.5
RMS_EPS = 1e-6
VMEM_LIMIT = 56 * 1024 * 1024
OUT_ROWS = 16


def _params(*sem):
    return pltpu.CompilerParams(dimension_semantics=sem, vmem_limit_bytes=VMEM_LIMIT)


def _rms(x, g):
    return x * lax.rsqrt(jnp.mean(x * x, axis=-1, keepdims=True) + RMS_EPS) * g


def _dot(a, b):
    return jnp.dot(a, b, preferred_element_type=F32)


def _dot_nt(a, b):
    return lax.dot_general(a, b, (((1,), (1,)), ((), ())), preferred_element_type=F32)


def _split_bf16(x):
    hi = x.astype(BF16)
    return hi, (x - hi.astype(F32)).astype(BF16)


def _norm_matmul_kernel(x_ref, g_ref, w_ref, *out_refs):
    u = _rms(x_ref[...], g_ref[...]).astype(BF16)
    width = out_refs[0].shape[-1]
    for i, o_ref in enumerate(out_refs):
        o_ref[...] = _dot(u, w_ref[:, i * width:(i + 1) * width])


def _norm_matmul(x, g, w, n_out, tm):
    m, d = x.shape
    width = w.shape[1] // n_out
    return pl.pallas_call(
        _norm_matmul_kernel,
        grid=(m // tm,),
        in_specs=[
            pl.BlockSpec((tm, d), lambda i: (i, 0)),
            pl.BlockSpec((1, d), lambda i: (0, 0)),
            pl.BlockSpec(w.shape, lambda i: (0, 0)),
        ],
        out_specs=[pl.BlockSpec((tm, width), lambda i: (i, 0))] * n_out,
        out_shape=[jax.ShapeDtypeStruct((m, width), F32)] * n_out,
        compiler_params=_params("parallel"),
        name="norm_matmul",
    )(x, g.reshape(1, d), w)


def _norm_matmul_t_kernel(x_ref, g_ref, w_ref, wt_ref, *out_refs, n_row):
    u = _rms(x_ref[...], g_ref[...]).astype(BF16)
    width = x_ref.shape[-1]
    for i, o_ref in enumerate(out_refs[:n_row]):
        o_ref[...] = _dot(u, w_ref[:, i * width:(i + 1) * width])
    for i, o_ref in enumerate(out_refs[n_row:]):
        o_ref[...] = _dot_nt(wt_ref[i * width:(i + 1) * width, :], u)


def _norm_matmul_t(x, g, w, wt, batch, seq, tm):
    m, d = x.shape
    n_row, n_t = w.shape[1] // d, wt.shape[0] // d
    per_seq = seq // tm
    return pl.pallas_call(
        functools.partial(_norm_matmul_t_kernel, n_row=n_row),
        grid=(m // tm,),
        in_specs=[
            pl.BlockSpec((tm, d), lambda i: (i, 0)),
            pl.BlockSpec((1, d), lambda i: (0, 0)),
            pl.BlockSpec(w.shape, lambda i: (0, 0)),
            pl.BlockSpec(wt.shape, lambda i: (0, 0)),
        ],
        out_specs=[pl.BlockSpec((tm, d), lambda i: (i, 0))] * n_row
        + [pl.BlockSpec((None, d, tm), lambda i: (i // per_seq, 0, i % per_seq))] * n_t,
        out_shape=[jax.ShapeDtypeStruct((m, d), F32)] * n_row
        + [jax.ShapeDtypeStruct((batch, d, seq), F32)] * n_t,
        compiler_params=_params("parallel"),
        name="norm_matmul_t",
    )(x, g.reshape(1, d), w, wt)


def _matmul_norm_res_kernel(a_ref, w_ref, g_ref, h_ref, o_ref):
    y = _dot(a_ref[...].astype(BF16), w_ref[...])
    o_ref[...] = h_ref[...] + _rms(y, g_ref[...])


def _matmul_norm_res(a, w, g, h, tm):
    m, k = a.shape
    d = w.shape[1]
    return pl.pallas_call(
        _matmul_norm_res_kernel,
        grid=(m // tm,),
        in_specs=[
            pl.BlockSpec((tm, k), lambda i: (i, 0)),
            pl.BlockSpec(w.shape, lambda i: (0, 0)),
            pl.BlockSpec((1, d), lambda i: (0, 0)),
            pl.BlockSpec((tm, d), lambda i: (i, 0)),
        ],
        out_specs=pl.BlockSpec((tm, d), lambda i: (i, 0)),
        out_shape=jax.ShapeDtypeStruct((m, d), F32),
        compiler_params=_params("parallel"),
        name="matmul_norm_res",
    )(a, w, g.reshape(1, d), h)


def _merge_matmul_norm_res_kernel(o0, o1, o2, l0, l1, l2, w_ref, g_ref, h_ref, out_ref):
    a0, a1, a2 = l0[...], l1[...], l2[...]
    mx = jnp.maximum(jnp.maximum(a0, a1), a2)
    e0, e1, e2 = jnp.exp(a0 - mx), jnp.exp(a1 - mx), jnp.exp(a2 - mx)
    den = e0 + e1 + e2
    merged = (e0 / den) * o0[...] + (e1 / den) * o1[...] + (e2 / den) * o2[...]
    y = _dot(merged.astype(BF16), w_ref[...])
    out_ref[...] = h_ref[...] + _rms(y, g_ref[...])


def _merge_matmul_norm_res(outs, lses, w, g, h, tm):
    m, d = h.shape
    tok = pl.BlockSpec((tm, d), lambda i: (i, 0))
    return pl.pallas_call(
        _merge_matmul_norm_res_kernel,
        grid=(m // tm,),
        in_specs=[tok] * 6 + [
            pl.BlockSpec(w.shape, lambda i: (0, 0)),
            pl.BlockSpec((1, d), lambda i: (0, 0)),
            tok,
        ],
        out_specs=tok,
        out_shape=jax.ShapeDtypeStruct((m, d), F32),
        compiler_params=_params("parallel"),
        name="merge_matmul_norm_res",
    )(*outs, *lses, w, g.reshape(1, d), h)


def _mlp_kernel(h_ref, g1_ref, win_ref, wout_ref, g2_ref, o_ref, *, chunk):
    h = h_ref[...]
    u = _rms(h, g1_ref[...]).astype(BF16)
    acc = jnp.zeros(h.shape, F32)
    for c in range(win_ref.shape[1] // chunk):
        a = _dot(u, win_ref[:, c * chunk:(c + 1) * chunk])
        a = jnp.square(jnp.maximum(a, 0.0)).astype(BF16)
        acc = acc + _dot(a, wout_ref[c * chunk:(c + 1) * chunk, :])
    o_ref[...] = h + _rms(acc, g2_ref[...])


def _mlp(h, g1, w_in, w_out, g2, tm):
    m, d = h.shape
    return pl.pallas_call(
        functools.partial(_mlp_kernel, chunk=1024),
        grid=(m // tm,),
        in_specs=[
            pl.BlockSpec((tm, d), lambda i: (i, 0)),
            pl.BlockSpec((1, d), lambda i: (0, 0)),
            pl.BlockSpec(w_in.shape, lambda i: (0, 0)),
            pl.BlockSpec(w_out.shape, lambda i: (0, 0)),
            pl.BlockSpec((1, d), lambda i: (0, 0)),
        ],
        out_specs=pl.BlockSpec((tm, d), lambda i: (i, 0)),
        out_shape=jax.ShapeDtypeStruct((m, d), F32),
        compiler_params=_params("parallel"),
        name="mlp",
    )(h, g1.reshape(1, d), w_in, w_out, g2.reshape(1, d))


def _sb_tile(z, later, tail, mask):
    t = jnp.log(1.0 + jnp.exp(-jnp.abs(z)))
    ell = -jnp.maximum(z, 0.0) - t
    lsig = jnp.minimum(z, 0.0) - t
    if mask is not None:
        ell = jnp.where(mask, ell, 0.0)
    rem = _dot(ell.astype(BF16), later) + tail
    a = jnp.exp(lsig + rem)
    if mask is not None:
        a = jnp.where(mask, a, 0.0)
    return a, jnp.sum(ell, axis=-1, keepdims=True)


def _later_matrix(n):
    row = lax.broadcasted_iota(jnp.int32, (n, n), 0)
    col = lax.broadcasted_iota(jnp.int32, (n, n), 1)
    return (row > col).astype(BF16)


def _sb_prompt_kernel(bias_ref, later_ref, q_ref, kt_ref, vt_ref, o_ref, *, tq):
    hp = pl.program_id(1)
    m = pl.program_id(2)
    wide = 2 * tq
    row = lax.broadcasted_iota(jnp.int32, (tq, wide), 0)
    col = lax.broadcasted_iota(jnp.int32, (tq, wide), 1)
    mask_lo = lax.broadcasted_iota(jnp.int32, (tq, tq), 1) < lax.broadcasted_iota(jnp.int32, (tq, tq), 0)
    mask_hi = col < row + tq
    later = later_ref[...]
    base = pl.multiple_of(m * wide, wide)
    heads = [(slice(hh * HEAD_DIM, (hh + 1) * HEAD_DIM), bias_ref[0, 2 * hp + hh]) for hh in range(2)]
    qs = [[(q_ref[part * tq:(part + 1) * tq, sl] * ATTN_SCALE).astype(BF16) for part in range(2)]
          for sl, _ in heads]

    def attend(qh, kth, vth, bias, later_kk, tail, acc, mask):
        a, ell_sum = _sb_tile(_dot(qh, kth) + bias, later_kk, tail, mask)
        return tail + ell_sum, acc + _dot_nt(a.astype(BF16), vth)

    zero = (jnp.zeros((tq, 1), F32), jnp.zeros((tq, HEAD_DIM), F32))
    carry = []
    for (sl, bias), (q_lo, q_hi) in zip(heads, qs):
        kth = kt_ref[sl, pl.ds(base, wide)].astype(BF16)
        vth = vt_ref[sl, pl.ds(base, wide)].astype(BF16)
        carry.append(attend(q_lo, kth[:, :tq], vth[:, :tq], bias, later[:tq, :tq], *zero, mask_lo))
        carry.append(attend(q_hi, kth, vth, bias, later, *zero, mask_hi))

    def sweep(jj, carry):
        start = pl.multiple_of((m - 1 - jj) * wide, wide)
        out = []
        for hh, ((sl, bias), q_parts) in enumerate(zip(heads, qs)):
            kth = kt_ref[sl, pl.ds(start, wide)].astype(BF16)
            vth = vt_ref[sl, pl.ds(start, wide)].astype(BF16)
            for part, qh in enumerate(q_parts):
                out.append(attend(qh, kth, vth, bias, later, *carry[2 * hh + part], None))
        return tuple(out)

    carry = lax.fori_loop(0, m, sweep, tuple(carry))
    for part in range(2):
        o_ref[part * tq:(part + 1) * tq, :] = jnp.concatenate([carry[part][1], carry[2 + part][1]], axis=-1)


def _sb_prompt(q, kt, vt, bias, batch, seq):
    d = q.shape[1]
    tq = 256
    wide = 2 * tq
    lanes = 2 * HEAD_DIM
    out = pl.pallas_call(
        functools.partial(_sb_prompt_kernel, tq=tq),
        grid=(batch, d // lanes, seq // wide),
        in_specs=[
            pl.BlockSpec(memory_space=pltpu.SMEM),
            pl.BlockSpec((wide, wide), lambda b, h, i: (0, 0)),
            pl.BlockSpec((None, wide, lanes), lambda b, h, i: (b, i, h)),
            pl.BlockSpec((None, lanes, seq), lambda b, h, i: (b, h, 0)),
            pl.BlockSpec((None, lanes, seq), lambda b, h, i: (b, h, 0)),
        ],
        out_specs=pl.BlockSpec((None, wide, lanes), lambda b, h, i: (b, i, h)),
        out_shape=jax.ShapeDtypeStruct((batch, seq, d), F32),
        compiler_params=_params("parallel", "parallel", "arbitrary"),
        name="sb_prompt",
    )(bias.reshape(1, -1), _later_matrix(wide), q.reshape(batch, seq, d), kt, vt)
    return out.reshape(batch * seq, d)


def _row_queries(q, batch, t_new):
    q4 = q.reshape(batch, t_new, N_HEADS, HEAD_DIM) * ATTN_SCALE
    eye = jnp.eye(N_HEADS, dtype=q.dtype)
    w = jnp.einsum('bthd,hg->bhtgd', q4, eye)
    return w.reshape(batch, N_HEADS * t_new, N_HEADS * HEAD_DIM).astype(BF16)


def _row_heads(x, t_new):
    return jnp.broadcast_to(jnp.repeat(x.astype(F32), t_new)[:, None], (N_HEADS * t_new, LANES))


def _new_keys_t(x, batch, t_new):
    xt = x.reshape(batch, t_new, N_HEADS * HEAD_DIM).transpose(0, 2, 1)
    return jnp.pad(xt, ((0, 0), (0, 0), (0, LANES - t_new)))


def _select_rows(t_new):
    r = jnp.arange(OUT_ROWS)[:, None]
    c = jnp.arange(N_HEADS * t_new)[None, :]
    return (c % t_new == r).astype(BF16)


def _diag_rows(acc, sel, t_new):
    row = lax.broadcasted_iota(jnp.int32, acc.shape, 0)
    col = lax.broadcasted_iota(jnp.int32, acc.shape, 1)
    own = jnp.where(row // t_new == col // HEAD_DIM, acc, 0.0)
    hi, lo = _split_bf16(own)
    return _dot(sel, hi) + _dot(sel, lo)


def _sb_sample_kernel(pt_ref, q_ref, bias_ref, later_ref, sel_ref, knew_ref, vnew_ref, *rest, pp, t_new):
    k_refs, v_refs = rest[:pp], rest[pp:2 * pp]
    o_ref, tail_ref, acc_ref = rest[2 * pp:]
    j = pl.program_id(1)
    q = q_ref[...]
    bias = bias_ref[:, 0:1]

    def block(kt, vt, later, mask):
        a, ell_sum = _sb_tile(_dot(q, kt) + bias, later, tail_ref[:, 0:1], mask)
        acc_ref[...] += _dot_nt(a.astype(BF16), vt)
        tail_ref[...] += ell_sum

    @pl.when(j == 0)
    def _():
        tail_ref[...] = jnp.zeros_like(tail_ref)
        acc_ref[...] = jnp.zeros_like(acc_ref)
        row = lax.broadcasted_iota(jnp.int32, (q.shape[0], LANES), 0)
        col = lax.broadcasted_iota(jnp.int32, (q.shape[0], LANES), 1)
        block(knew_ref[...].astype(BF16), vnew_ref[...].astype(BF16),
              later_ref[0:LANES, 0:LANES], col < row % t_new)

    kt = jnp.concatenate([r[...].astype(BF16) for r in k_refs], axis=1)
    vt = jnp.concatenate([r[...].astype(BF16) for r in v_refs], axis=1)
    block(kt, vt, later_ref[...], None)

    @pl.when(j == pl.num_programs(1) - 1)
    def _():
        o_ref[...] = _diag_rows(acc_ref[...], sel_ref[...], t_new)


def _sb_sample(q, k_new, v_new, cache_k, cache_v, page_table, bias, layer, batch, t_new):
    n_layers, n_phys, page, nh, hd = cache_k.shape
    d = nh * hd
    n_pages = page_table.shape[1]
    pp = 8
    rows = nh * t_new
    ck = cache_k.transpose(0, 1, 3, 4, 2).reshape(n_layers * n_phys, d, page)
    cv = cache_v.transpose(0, 1, 3, 4, 2).reshape(n_layers * n_phys, d, page)

    def page_spec(p):
        def index(b, j, pt):
            return (layer * n_phys + pt[b, n_pages - (j + 1) * pp + p], 0, 0)
        return pl.BlockSpec((None, d, page), index)

    const2 = lambda b, j, pt: (0, 0)
    per_b = lambda b, j, pt: (b, 0, 0)
    out = pl.pallas_call(
        functools.partial(_sb_sample_kernel, pp=pp, t_new=t_new),
        grid_spec=pltpu.PrefetchScalarGridSpec(
            num_scalar_prefetch=1,
            grid=(batch, n_pages // pp),
            in_specs=[
                pl.BlockSpec((None, rows, d), per_b),
                pl.BlockSpec((rows, LANES), const2),
                pl.BlockSpec((pp * page, pp * page), const2),
                pl.BlockSpec((OUT_ROWS, rows), const2),
                pl.BlockSpec((None, d, LANES), per_b),
                pl.BlockSpec((None, d, LANES), per_b),
            ] + [page_spec(p) for p in range(pp)] * 2,
            out_specs=pl.BlockSpec((None, OUT_ROWS, d), per_b),
            scratch_shapes=[pltpu.VMEM((rows, LANES), F32), pltpu.VMEM((rows, d), F32)],
        ),
        out_shape=jax.ShapeDtypeStruct((batch, OUT_ROWS, d), F32),
        compiler_params=_params("parallel", "arbitrary"),
        name="sb_sample",
    )(page_table, _row_queries(q, batch, t_new), _row_heads(bias, t_new), _later_matrix(pp * page),
      _select_rows(t_new), _new_keys_t(k_new, batch, t_new), _new_keys_t(v_new, batch, t_new),
      *([ck] * pp), *([cv] * pp))
    return out[:, :t_new].reshape(batch * t_new, d)


def _alibi_slopes(n_heads):
    return 2.0 ** (-8.0 * (jnp.arange(n_heads, dtype=F32) + 1.0) / n_heads)


def _dil_prompt_kernel(slope_ref, q_ref, kc_ref, kp_ref, vc_ref, vp_ref, o_ref, l_ref, *, dil):
    n = pl.program_id(2)
    row = lax.broadcasted_iota(jnp.int32, (BLOCK, BLOCK), 0)
    col = lax.broadcasted_iota(jnp.int32, (BLOCK, BLOCK), 1)
    delta_c = row - col
    delta_p = delta_c + BLOCK
    valid_c = delta_c >= 0
    valid_p = jnp.logical_and(delta_p <= BLOCK, n > 0)
    dist_c = (delta_c * dil).astype(F32)
    dist_p = (delta_p * dil).astype(F32)
    for h in range(N_HEADS):
        sl = slice(h * HEAD_DIM, (h + 1) * HEAD_DIM)
        slope = slope_ref[0, h]
        qh = (q_ref[:, sl] * ATTN_SCALE).astype(BF16)
        s_c = _dot_nt(qh, kc_ref[:, sl].astype(BF16)) - slope * dist_c
        s_p = _dot_nt(qh, kp_ref[:, sl].astype(BF16)) - slope * dist_p
        s_c = jnp.where(valid_c, s_c, -jnp.inf)
        s_p = jnp.where(valid_p, s_p, -jnp.inf)
        m = jnp.maximum(jnp.max(s_c, axis=-1, keepdims=True), jnp.max(s_p, axis=-1, keepdims=True))
        p_c = jnp.exp(s_c - m)
        p_p = jnp.exp(s_p - m)
        l = jnp.sum(p_c, axis=-1, keepdims=True) + jnp.sum(p_p, axis=-1, keepdims=True)
        o = _dot(p_c.astype(BF16), vc_ref[:, sl].astype(BF16)) + _dot(p_p.astype(BF16), vp_ref[:, sl].astype(BF16))
        o_ref[:, sl] = o / l
        l_ref[:, sl] = jnp.broadcast_to(m + jnp.log(l), (BLOCK, HEAD_DIM))


def _dil_prompt_branch(q, k, v, slopes, dil, batch, seq):
    d = q.shape[1]
    length = seq // dil
    nb = length // BLOCK

    def view(a):
        return a.reshape(batch, length, dil * d)

    cur = pl.BlockSpec((None, BLOCK, d), lambda b, r, n: (b, n, r))
    prev = pl.BlockSpec((None, BLOCK, d), lambda b, r, n: (b, jnp.maximum(n - 1, 0), r))
    o, lse = pl.pallas_call(
        functools.partial(_dil_prompt_kernel, dil=dil),
        grid=(batch, dil, nb),
        in_specs=[pl.BlockSpec(memory_space=pltpu.SMEM), cur, cur, prev, cur, prev],
        out_specs=[cur, cur],
        out_shape=[jax.ShapeDtypeStruct((batch, length, dil * d), F32)] * 2,
        compiler_params=_params("parallel", "parallel", "arbitrary"),
        name=f"dil_prompt_{dil}",
    )(slopes.reshape(1, -1), view(q), view(k), view(k), view(v), view(v))
    return o.reshape(batch * seq, d), lse.reshape(batch * seq, d)


def _dil_sample_kernel(q_ref, slope_ref, sel_ref, knew_ref, vnew_ref, kt_ref, vt_ref, o_ref, *, t_new):
    n_buf = kt_ref.shape[1]
    rows = q_ref.shape[1]
    slope = slope_ref[:, 0:1]
    knew = knew_ref[...].astype(BF16)
    vnew = vnew_ref[...].astype(BF16)
    outs, lses, ls = [], [], []
    for g, (win, dil) in enumerate(BRANCHES):
        q = q_ref[g]
        t_q = lax.broadcasted_iota(jnp.int32, (rows, win), 0) % t_new
        dist = win + t_q - lax.broadcasted_iota(jnp.int32, (rows, win), 1)
        valid = jnp.logical_and(jnp.bitwise_and(dist, dil - 1) == 0, dist <= win)
        s = _dot(q, kt_ref[:, n_buf - win:].astype(BF16)) - slope * dist.astype(F32)
        s = jnp.where(valid, s, -jnp.inf)
        t_n = lax.broadcasted_iota(jnp.int32, (rows, LANES), 0) % t_new
        dist_n = t_n - lax.broadcasted_iota(jnp.int32, (rows, LANES), 1)
        valid_n = jnp.logical_and(jnp.bitwise_and(dist_n, dil - 1) == 0, dist_n >= 0)
        s_n = _dot(q, knew) - slope * dist_n.astype(F32)
        s_n = jnp.where(valid_n, s_n, -jnp.inf)
        m = jnp.maximum(jnp.max(s, axis=-1, keepdims=True), jnp.max(s_n, axis=-1, keepdims=True))
        p = jnp.exp(s - m)
        p_n = jnp.exp(s_n - m)
        l = jnp.sum(p, axis=-1, keepdims=True) + jnp.sum(p_n, axis=-1, keepdims=True)
        outs.append(_dot_nt(p.astype(BF16), vt_ref[:, n_buf - win:].astype(BF16)) + _dot_nt(p_n.astype(BF16), vnew))
        lses.append(m + jnp.log(l))
        ls.append(l)
    mx = jnp.maximum(jnp.maximum(lses[0], lses[1]), lses[2])
    es = [jnp.exp(x - mx) for x in lses]
    den = es[0] + es[1] + es[2]
    merged = sum((e / den / l) * o for e, l, o in zip(es, ls, outs))
    o_ref[...] = _diag_rows(merged, sel_ref[...], t_new)


def _dil_sample(qg, k_new, v_new, cache_k, cache_v, slopes, batch, t_new):
    _, n_buf, nh, hd = cache_k.shape
    d = nh * hd
    rows = nh * t_new
    assert n_buf >= BRANCHES[-1][0] and t_new <= min(dil for _, dil in BRANCHES[1:])
    q3 = jnp.stack([_row_queries(q, batch, t_new) for q in qg], axis=1)
    kt = cache_k.transpose(0, 2, 3, 1).reshape(batch, d, n_buf)
    vt = cache_v.transpose(0, 2, 3, 1).reshape(batch, d, n_buf)
    new = pl.BlockSpec((None, d, LANES), lambda b: (b, 0, 0))
    buf = pl.BlockSpec((None, d, n_buf), lambda b: (b, 0, 0))
    out = pl.pallas_call(
        functools.partial(_dil_sample_kernel, t_new=t_new),
        grid=(batch,),
        in_specs=[
            pl.BlockSpec((None, len(BRANCHES), rows, d), lambda b: (b, 0, 0, 0)),
            pl.BlockSpec((rows, LANES), lambda b: (0, 0)),
            pl.BlockSpec((OUT_ROWS, rows), lambda b: (0, 0)),
            new, new, buf, buf,
        ],
        out_specs=pl.BlockSpec((None, OUT_ROWS, d), lambda b: (b, 0, 0)),
        out_shape=jax.ShapeDtypeStruct((batch, OUT_ROWS, d), F32),
        compiler_params=_params("parallel"),
        name="dil_sample",
    )(q3, _row_heads(slopes, t_new), _select_rows(t_new), _new_keys_t(k_new, batch, t_new),
      _new_keys_t(v_new, batch, t_new), kt, vt)
    return out[:, :t_new].reshape(batch * t_new, d)


def _shift_kernel(ck_ref, k_new_ref, cv_ref, v_new_ref, ok_ref, ov_ref, *, t_new):
    for c_ref, new_ref, o_ref in ((ck_ref, k_new_ref, ok_ref), (cv_ref, v_new_ref, ov_ref)):
        o_ref[...] = jnp.concatenate([c_ref[:, t_new:], new_ref[:, :t_new]], axis=1)


def _shift_window(cache_k, cache_v, k_new, v_new, batch, t_new):
    _, n_buf, nh, hd = cache_k.shape
    d = nh * hd
    rows = 256
    kt = cache_k.transpose(0, 2, 3, 1).reshape(batch, d, n_buf)
    vt = cache_v.transpose(0, 2, 3, 1).reshape(batch, d, n_buf)
    buf = pl.BlockSpec((None, rows, n_buf), lambda b, j: (b, j, 0))
    new = pl.BlockSpec((None, rows, LANES), lambda b, j: (b, j, 0))
    ok, ov = pl.pallas_call(
        functools.partial(_shift_kernel, t_new=t_new),
        grid=(batch, d // rows),
        in_specs=[buf, new, buf, new],
        out_specs=[buf, buf],
        out_shape=[jax.ShapeDtypeStruct((batch, d, n_buf), cache_k.dtype)] * 2,
        compiler_params=_params("parallel", "parallel"),
        name="shift_window",
    )(kt, _new_keys_t(k_new, batch, t_new), vt, _new_keys_t(v_new, batch, t_new))
    back = lambda a: a.reshape(batch, nh, hd, n_buf).transpose(0, 3, 1, 2)
    return back(ok), back(ov)


def kernel(x_prompt, x_sample, cache_k_a, cache_v_a, cache_k_b, cache_v_b, page_table, g_pre_mix, g_post_mix, g_pre_mlp, g_post_mlp, w_qkv_a, w_o_a, b_sb, w_q_b, w_o_b, g_kv, w_kv_b, w_mlp_in, w_mlp_out):
    batch, seq, d = x_prompt.shape
    dec_b, t_new, _ = x_sample.shape
    depth = g_pre_mix.shape[0]
    n_a = w_qkv_a.shape[0]
    assert depth == 2 and n_a == 1 and seq <= BRANCHES[2][0] and d == N_HEADS * HEAD_DIM
    tm_p, tm_s = 512, dec_b * t_new
    slopes = _alibi_slopes(N_HEADS)
    bf = lambda w: w.astype(BF16)
    heads = (N_HEADS, HEAD_DIM)
    from_t = lambda a: a.reshape(a.shape[0], *heads, a.shape[-1]).transpose(0, 3, 1, 2)

    xp = x_prompt.reshape(batch * seq, d)
    xs = x_sample.reshape(dec_b * t_new, d)

    wqkv = bf(w_qkv_a[0])
    qp, kpt, vpt = _norm_matmul_t(xp, g_pre_mix[0], wqkv[:, :d], wqkv[:, d:].T, batch, seq, tm_p)
    qs, ks, vs = _norm_matmul(xs, g_pre_mix[0], wqkv, 3, tm_s)
    ap = _sb_prompt(qp, kpt, vpt, b_sb[0], batch, seq)
    as_ = _sb_sample(qs, ks, vs, cache_k_a, cache_v_a, page_table, b_sb[0], 0, dec_b, t_new)
    wo = bf(w_o_a[0])
    hp = _matmul_norm_res(ap, wo, g_post_mix[0], xp, tm_p)
    hs = _matmul_norm_res(as_, wo, g_post_mix[0], xs, tm_s)
    w_in, w_out = bf(w_mlp_in[0]), bf(w_mlp_out[0])
    hp = _mlp(hp, g_pre_mlp[0], w_in, w_out, g_post_mlp[0], 256)
    hs = _mlp(hs, g_pre_mlp[0], w_in, w_out, g_post_mlp[0], tm_s)

    wkv = bf(w_kv_b)
    kbp, vbp, kbpt, vbpt = _norm_matmul_t(hp, g_kv, wkv, wkv.T, batch, seq, tm_p)
    kbs, vbs = _norm_matmul(hs, g_kv, wkv, 2, tm_s)

    wq = bf(w_q_b[0])
    qgp = _norm_matmul(hp, g_pre_mix[1], wq, 3, tm_p)
    qgs = _norm_matmul(hs, g_pre_mix[1], wq, 3, tm_s)
    outs, lses = [], []
    for g, (win, dil) in enumerate(BRANCHES):
        assert win // dil == BLOCK
        o, l = _dil_prompt_branch(qgp[g], kbp, vbp, slopes, dil, batch, seq)
        outs.append(o)
        lses.append(l)
    wo = bf(w_o_b[0])
    hp = _merge_matmul_norm_res(outs, lses, wo, g_post_mix[1], hp, tm_p)
    ms = _dil_sample(qgs, kbs, vbs, cache_k_b, cache_v_b, slopes, dec_b, t_new)
    hs = _matmul_norm_res(ms, wo, g_post_mix[1], hs, tm_s)
    w_in, w_out = bf(w_mlp_in[1]), bf(w_mlp_out[1])
    hp = _mlp(hp, g_pre_mlp[1], w_in, w_out, g_post_mlp[1], 256)
    hs = _mlp(hs, g_pre_mlp[1], w_in, w_out, g_post_mlp[1], tm_s)

    k_b_sample, v_b_sample = _shift_window(cache_k_b, cache_v_b, kbs, vbs, dec_b, t_new)
    keep = min(BRANCHES[-1][0], seq)
    return (
        hp.reshape(batch, seq, d),
        hs.reshape(dec_b, t_new, d),
        from_t(kpt)[None],
        from_t(vpt)[None],
        ks.reshape(1, dec_b, t_new, *heads),
        vs.reshape(1, dec_b, t_new, *heads),
        from_t(kbpt)[:, seq - keep:],
        from_t(vbpt)[:, seq - keep:],
        k_b_sample,
        v_b_sample,
    )
```

```python
import functools

import jax
import jax.numpy as jnp
from jax import lax
from jax.experimental import pallas as pl
from jax.experimental.pallas import tpu as pltpu

F32 = jnp.float32
BF16 = jnp.bfloat16

HEAD_DIM = 64
N_HEADS = 16
BRANCHES = ((128, 1), (512, 4), (2048, 16))
BLOCK = 128
LANES = 128
ATTN_SCALE = HEAD_DIM ** -0.5
RMS_EPS = 1e-6
VMEM_LIMIT = 56 * 1024 * 1024
OUT_ROWS = 16


def _params(*sem):
    return pltpu.CompilerParams(dimension_semantics=sem, vmem_limit_bytes=VMEM_LIMIT)


def _rms(x, g):
    return x * lax.rsqrt(jnp.mean(x * x, axis=-1, keepdims=True) + RMS_EPS) * g


def _dot(a, b):
    return jnp.dot(a, b, preferred_element_type=F32)


def _dot_nt(a, b):
    return lax.dot_general(a, b, (((1,), (1,)), ((), ())), preferred_element_type=F32)


def _split_bf16(x):
    hi = x.astype(BF16)
    return hi, (x - hi.astype(F32)).astype(BF16)


def _norm_matmul_kernel(x_ref, g_ref, w_ref, *out_refs):
    u = _rms(x_ref[...], g_ref[...]).astype(BF16)
    width = out_refs[0].shape[-1]
    for i, o_ref in enumerate(out_refs):
        o_ref[...] = _dot(u, w_ref[:, i * width:(i + 1) * width])


def _norm_matmul(x, g, w, n_out, tm):
    m, d = x.shape
    width = w.shape[1] // n_out
    return pl.pallas_call(
        _norm_matmul_kernel,
        grid=(m // tm,),
        in_specs=[
            pl.BlockSpec((tm, d), lambda i: (i, 0)),
            pl.BlockSpec((1, d), lambda i: (0, 0)),
            pl.BlockSpec(w.shape, lambda i: (0, 0)),
        ],
        out_specs=[pl.BlockSpec((tm, width), lambda i: (i, 0))] * n_out,
        out_shape=[jax.ShapeDtypeStruct((m, width), F32)] * n_out,
        compiler_params=_params("parallel"),
        name="norm_matmul",
    )(x, g.reshape(1, d), w)


def _norm_matmul_t_kernel(x_ref, g_ref, w_ref, wt_ref, *out_refs, n_row):
    u = _rms(x_ref[...], g_ref[...]).astype(BF16)
    width = x_ref.shape[-1]
    for i, o_ref in enumerate(out_refs[:n_row]):
        o_ref[...] = _dot(u, w_ref[:, i * width:(i + 1) * width])
    for i, o_ref in enumerate(out_refs[n_row:]):
        o_ref[...] = _dot_nt(wt_ref[i * width:(i + 1) * width, :], u)


def _norm_matmul_t(x, g, w, wt, batch, seq, tm):
    m, d = x.shape
    n_row, n_t = w.shape[1] // d, wt.shape[0] // d
    per_seq = seq // tm
    return pl.pallas_call(
        functools.partial(_norm_matmul_t_kernel, n_row=n_row),
        grid=(m // tm,),
        in_specs=[
            pl.BlockSpec((tm, d), lambda i: (i, 0)),
            pl.BlockSpec((1, d), lambda i: (0, 0)),
            pl.BlockSpec(w.shape, lambda i: (0, 0)),
            pl.BlockSpec(wt.shape, lambda i: (0, 0)),
        ],
        out_specs=[pl.BlockSpec((tm, d), lambda i: (i, 0))] * n_row
        + [pl.BlockSpec((None, d, tm), lambda i: (i // per_seq, 0, i % per_seq))] * n_t,
        out_shape=[jax.ShapeDtypeStruct((m, d), F32)] * n_row
        + [jax.ShapeDtypeStruct((batch, d, seq), F32)] * n_t,
        compiler_params=_params("parallel"),
        name="norm_matmul_t",
    )(x, g.reshape(1, d), w, wt)


def _matmul_norm_res_kernel(a_ref, w_ref, g_ref, h_ref, o_ref):
    y = _dot(a_ref[...].astype(BF16), w_ref[...])
    o_ref[...] = h_ref[...] + _rms(y, g_ref[...])


def _matmul_norm_res(a, w, g, h, tm):
    m, k = a.shape
    d = w.shape[1]
    return pl.pallas_call(
        _matmul_norm_res_kernel,
        grid=(m // tm,),
        in_specs=[
            pl.BlockSpec((tm, k), lambda i: (i, 0)),
            pl.BlockSpec(w.shape, lambda i: (0, 0)),
            pl.BlockSpec((1, d), lambda i: (0, 0)),
            pl.BlockSpec((tm, d), lambda i: (i, 0)),
        ],
        out_specs=pl.BlockSpec((tm, d), lambda i: (i, 0)),
        out_shape=jax.ShapeDtypeStruct((m, d), F32),
        compiler_params=_params("parallel"),
        name="matmul_norm_res",
    )(a, w, g.reshape(1, d), h)


def _mlp_kernel(h_ref, g1_ref, win_ref, wout_ref, g2_ref, o_ref, *, chunk):
    h = h_ref[...]
    u = _rms(h, g1_ref[...]).astype(BF16)
    acc = jnp.zeros(h.shape, F32)
    for c in range(win_ref.shape[1] // chunk):
        a = _dot(u, win_ref[:, c * chunk:(c + 1) * chunk])
        a = jnp.square(jnp.maximum(a, 0.0)).astype(BF16)
        acc = acc + _dot(a, wout_ref[c * chunk:(c + 1) * chunk, :])
    o_ref[...] = h + _rms(acc, g2_ref[...])


def _mlp(h, g1, w_in, w_out, g2, tm):
    m, d = h.shape
    return pl.pallas_call(
        functools.partial(_mlp_kernel, chunk=1024),
        grid=(m // tm,),
        in_specs=[
            pl.BlockSpec((tm, d), lambda i: (i, 0)),
            pl.BlockSpec((1, d), lambda i: (0, 0)),
            pl.BlockSpec(w_in.shape, lambda i: (0, 0)),
            pl.BlockSpec(w_out.shape, lambda i: (0, 0)),
            pl.BlockSpec((1, d), lambda i: (0, 0)),
        ],
        out_specs=pl.BlockSpec((tm, d), lambda i: (i, 0)),
        out_shape=jax.ShapeDtypeStruct((m, d), F32),
        compiler_params=_params("parallel"),
        name="mlp",
    )(h, g1.reshape(1, d), w_in, w_out, g2.reshape(1, d))


def _sb_tile(z, later, tail, mask):
    t = jnp.log(1.0 + jnp.exp(-jnp.abs(z)))
    ell = -jnp.maximum(z, 0.0) - t
    lsig = jnp.minimum(z, 0.0) - t
    if mask is not None:
        ell = jnp.where(mask, ell, 0.0)
    rem = _dot(ell.astype(BF16), later) + tail
    a = jnp.exp(lsig + rem)
    if mask is not None:
        a = jnp.where(mask, a, 0.0)
    return a, jnp.sum(ell, axis=-1, keepdims=True)


def _later_matrix(n):
    row = lax.broadcasted_iota(jnp.int32, (n, n), 0)
    col = lax.broadcasted_iota(jnp.int32, (n, n), 1)
    return (row > col).astype(BF16)


def _sb_prompt_kernel(bias_ref, later_ref, q_ref, kt_ref, vt_ref, o_ref, *, tq):
    hp = pl.program_id(1)
    m = pl.program_id(2)
    wide = 2 * tq
    row = lax.broadcasted_iota(jnp.int32, (tq, wide), 0)
    col = lax.broadcasted_iota(jnp.int32, (tq, wide), 1)
    mask_lo = lax.broadcasted_iota(jnp.int32, (tq, tq), 1) < lax.broadcasted_iota(jnp.int32, (tq, tq), 0)
    mask_hi = col < row + tq
    later = later_ref[...]
    base = pl.multiple_of(m * wide, wide)
    heads = [(slice(hh * HEAD_DIM, (hh + 1) * HEAD_DIM), bias_ref[0, 2 * hp + hh]) for hh in range(2)]
    qs = [[(q_ref[part * tq:(part + 1) * tq, sl] * ATTN_SCALE).astype(BF16) for part in range(2)]
          for sl, _ in heads]

    def attend(qh, kth, vth, bias, later_kk, tail, acc, mask):
        a, ell_sum = _sb_tile(_dot(qh, kth) + bias, later_kk, tail, mask)
        return tail + ell_sum, acc + _dot_nt(a.astype(BF16), vth)

    zero = (jnp.zeros((tq, 1), F32), jnp.zeros((tq, HEAD_DIM), F32))
    carry = []
    for (sl, bias), (q_lo, q_hi) in zip(heads, qs):
        kth = kt_ref[sl, pl.ds(base, wide)].astype(BF16)
        vth = vt_ref[sl, pl.ds(base, wide)].astype(BF16)
        carry.append(attend(q_lo, kth[:, :tq], vth[:, :tq], bias, later[:tq, :tq], *zero, mask_lo))
        carry.append(attend(q_hi, kth, vth, bias, later, *zero, mask_hi))

    def sweep(jj, carry):
        start = pl.multiple_of((m - 1 - jj) * wide, wide)
        out = []
        for hh, ((sl, bias), q_parts) in enumerate(zip(heads, qs)):
            kth = kt_ref[sl, pl.ds(start, wide)].astype(BF16)
            vth = vt_ref[sl, pl.ds(start, wide)].astype(BF16)
            for part, qh in enumerate(q_parts):
                out.append(attend(qh, kth, vth, bias, later, *carry[2 * hh + part], None))
        return tuple(out)

    carry = lax.fori_loop(0, m, sweep, tuple(carry))
    for part in range(2):
        o_ref[part * tq:(part + 1) * tq, :] = jnp.concatenate([carry[part][1], carry[2 + part][1]], axis=-1)


def _sb_prompt(q, kt, vt, bias, batch, seq):
    d = q.shape[1]
    tq = 256
    wide = 2 * tq
    lanes = 2 * HEAD_DIM
    out = pl.pallas_call(
        functools.partial(_sb_prompt_kernel, tq=tq),
        grid=(batch, d // lanes, seq // wide),
        in_specs=[
            pl.BlockSpec(memory_space=pltpu.SMEM),
            pl.BlockSpec((wide, wide), lambda b, h, i: (0, 0)),
            pl.BlockSpec((None, wide, lanes), lambda b, h, i: (b, i, h)),
            pl.BlockSpec((None, lanes, seq), lambda b, h, i: (b, h, 0)),
            pl.BlockSpec((None, lanes, seq), lambda b, h, i: (b, h, 0)),
        ],
        out_specs=pl.BlockSpec((None, wide, lanes), lambda b, h, i: (b, i, h)),
        out_shape=jax.ShapeDtypeStruct((batch, seq, d), F32),
        compiler_params=_params("parallel", "parallel", "arbitrary"),
        name="sb_prompt",
    )(bias.reshape(1, -1), _later_matrix(wide), q.reshape(batch, seq, d), kt, vt)
    return out.reshape(batch * seq, d)


def _row_queries(q, batch, t_new):
    q4 = q.reshape(batch, t_new, N_HEADS, HEAD_DIM) * ATTN_SCALE
    eye = jnp.eye(N_HEADS, dtype=q.dtype)
    w = jnp.einsum('bthd,hg->bhtgd', q4, eye)
    return w.reshape(batch, N_HEADS * t_new, N_HEADS * HEAD_DIM).astype(BF16)


def _row_heads(x, t_new):
    return jnp.broadcast_to(jnp.repeat(x.astype(F32), t_new)[:, None], (N_HEADS * t_new, LANES))


def _new_keys_t(x, batch, t_new):
    xt = x.reshape(batch, t_new, N_HEADS * HEAD_DIM).transpose(0, 2, 1)
    return jnp.pad(xt, ((0, 0), (0, 0), (0, LANES - t_new)))


def _select_rows(t_new):
    r = jnp.arange(OUT_ROWS)[:, None]
    c = jnp.arange(N_HEADS * t_new)[None, :]
    return (c % t_new == r).astype(BF16)


def _diag_rows(acc, sel, t_new):
    row = lax.broadcasted_iota(jnp.int32, acc.shape, 0)
    col = lax.broadcasted_iota(jnp.int32, acc.shape, 1)
    own = jnp.where(row // t_new == col // HEAD_DIM, acc, 0.0)
    hi, lo = _split_bf16(own)
    return _dot(sel, hi) + _dot(sel, lo)


def _sb_sample_kernel(pt_ref, q_ref, bias_ref, later_ref, sel_ref, knew_ref, vnew_ref, *rest, pp, t_new):
    k_refs, v_refs = rest[:pp], rest[pp:2 * pp]
    o_ref, tail_ref, acc_ref = rest[2 * pp:]
    j = pl.program_id(1)
    q = q_ref[...]
    bias = bias_ref[:, 0:1]

    def block(kt, vt, later, mask):
        a, ell_sum = _sb_tile(_dot(q, kt) + bias, later, tail_ref[:, 0:1], mask)
        acc_ref[...] += _dot_nt(a.astype(BF16), vt)
        tail_ref[...] += ell_sum

    @pl.when(j == 0)
    def _():
        tail_ref[...] = jnp.zeros_like(tail_ref)
        acc_ref[...] = jnp.zeros_like(acc_ref)
        row = lax.broadcasted_iota(jnp.int32, (q.shape[0], LANES), 0)
        col = lax.broadcasted_iota(jnp.int32, (q.shape[0], LANES), 1)
        block(knew_ref[...].astype(BF16), vnew_ref[...].astype(BF16),
              later_ref[0:LANES, 0:LANES], col < row % t_new)

    kt = jnp.concatenate([r[...].astype(BF16) for r in k_refs], axis=1)
    vt = jnp.concatenate([r[...].astype(BF16) for r in v_refs], axis=1)
    block(kt, vt, later_ref[...], None)

    @pl.when(j == pl.num_programs(1) - 1)
    def _():
        o_ref[...] = _diag_rows(acc_ref[...], sel_ref[...], t_new)


def _sb_sample(q, k_new, v_new, cache_k, cache_v, page_table, bias, layer, batch, t_new):
    n_layers, n_phys, page, nh, hd = cache_k.shape
    d = nh * hd
    n_pages = page_table.shape[1]
    pp = 8
    rows = nh * t_new
    ck = cache_k.transpose(0, 1, 3, 4, 2).reshape(n_layers * n_phys, d, page)
    cv = cache_v.transpose(0, 1, 3, 4, 2).reshape(n_layers * n_phys, d, page)

    def page_spec(p):
        def index(b, j, pt):
            return (layer * n_phys + pt[b, n_pages - (j + 1) * pp + p], 0, 0)
        return pl.BlockSpec((None, d, page), index)

    const2 = lambda b, j, pt: (0, 0)
    per_b = lambda b, j, pt: (b, 0, 0)
    out = pl.pallas_call(
        functools.partial(_sb_sample_kernel, pp=pp, t_new=t_new),
        grid_spec=pltpu.PrefetchScalarGridSpec(
            num_scalar_prefetch=1,
            grid=(batch, n_pages // pp),
            in_specs=[
                pl.BlockSpec((None, rows, d), per_b),
                pl.BlockSpec((rows, LANES), const2),
                pl.BlockSpec((pp * page, pp * page), const2),
                pl.BlockSpec((OUT_ROWS, rows), const2),
                pl.BlockSpec((None, d, LANES), per_b),
                pl.BlockSpec((None, d, LANES), per_b),
            ] + [page_spec(p) for p in range(pp)] * 2,
            out_specs=pl.BlockSpec((None, OUT_ROWS, d), per_b),
            scratch_shapes=[pltpu.VMEM((rows, LANES), F32), pltpu.VMEM((rows, d), F32)],
        ),
        out_shape=jax.ShapeDtypeStruct((batch, OUT_ROWS, d), F32),
        compiler_params=_params("parallel", "arbitrary"),
        name="sb_sample",
    )(page_table, _row_queries(q, batch, t_new), _row_heads(bias, t_new), _later_matrix(pp * page),
      _select_rows(t_new), _new_keys_t(k_new, batch, t_new), _new_keys_t(v_new, batch, t_new),
      *([ck] * pp), *([cv] * pp))
    return out[:, :t_new].reshape(batch * t_new, d)


def _alibi_slopes(n_heads):
    return 2.0 ** (-8.0 * (jnp.arange(n_heads, dtype=F32) + 1.0) / n_heads)


def _dil_prompt_kernel(slope_ref, q0_ref, q1_ref, q2_ref, k_ref, v_ref, out_ref, *scratch, seq):
    o_scr, l_scr = scratch[:len(BRANCHES)], scratch[len(BRANCHES):]
    hp = pl.program_id(1)
    row = lax.broadcasted_iota(jnp.int32, (BLOCK, BLOCK), 0)
    col = lax.broadcasted_iota(jnp.int32, (BLOCK, BLOCK), 1)
    delta_c = row - col
    delta_p = delta_c + BLOCK
    valid_c = delta_c >= 0
    band_p = delta_p <= BLOCK
    heads = [(col // HEAD_DIM == hh, slope_ref[0, 2 * hp + hh]) for hh in range(2)]
    n_blocks = seq // BLOCK
    per_trip = 4
    for g, ((_, dil), q_ref) in enumerate(zip(BRANCHES, (q0_ref, q1_ref, q2_ref))):
        nb = seq // (dil * BLOCK)
        dist_c = (delta_c * dil).astype(F32)
        dist_p = (delta_p * dil).astype(F32)

        def rows_of(r, n, dil=dil):
            start = n * (BLOCK * dil) + r
            return pl.ds(start, BLOCK) if dil == 1 else pl.ds(start, BLOCK, stride=dil)

        def load(idx, q_ref=q_ref, nb=nb, rows_of=rows_of):
            r, n = idx // nb, idx % nb
            cur = rows_of(r, n)
            data = [q_ref[cur, :], k_ref[cur, :], v_ref[cur, :]]
            if nb > 1:
                prev = rows_of(r, jnp.maximum(n - 1, 0))
                data += [k_ref[prev, :], v_ref[prev, :]]
            return cur, n, data

        def attend(n, data, nb=nb, dist_c=dist_c, dist_p=dist_p):
            q = data[0] * ATTN_SCALE
            kc, vc = data[1].astype(BF16), data[2].astype(BF16)
            if nb > 1:
                kp, vp = data[3].astype(BF16), data[4].astype(BF16)
                valid_p = jnp.logical_and(band_p, n > 0)
            outs, lses = [], []
            for own, slope in heads:
                qh = jnp.where(own, q, 0.0).astype(BF16)
                s_c = jnp.where(valid_c, _dot_nt(qh, kc) - slope * dist_c, -jnp.inf)
                m = jnp.max(s_c, axis=-1, keepdims=True)
                if nb > 1:
                    s_p = jnp.where(valid_p, _dot_nt(qh, kp) - slope * dist_p, -jnp.inf)
                    m = jnp.maximum(m, jnp.max(s_p, axis=-1, keepdims=True))
                p_c = jnp.exp(s_c - m)
                l = jnp.sum(p_c, axis=-1, keepdims=True)
                o = _dot(p_c.astype(BF16), vc)
                if nb > 1:
                    p_p = jnp.exp(s_p - m)
                    l = l + jnp.sum(p_p, axis=-1, keepdims=True)
                    o = o + _dot(p_p.astype(BF16), vp)
                outs.append(o / l)
                lses.append(m + jnp.log(l))
            return jnp.where(heads[0][0], outs[0], outs[1]), jnp.where(heads[0][0], lses[0], lses[1])

        def blocks(i, carry, g=g, load=load, attend=attend):
            loaded = [load(i + u * (n_blocks // per_trip)) for u in range(per_trip)]
            results = [(cur, attend(n, data)) for cur, n, data in loaded]
            for cur, (o, lse) in results:
                o_scr[g][cur, :] = o
                l_scr[g][cur, :] = lse
            return carry

        lax.fori_loop(0, n_blocks // per_trip, blocks, 0)

    chunk = 2 * BLOCK

    def merge(c, carry):
        rows = pl.ds(pl.multiple_of(c * chunk, chunk), chunk)
        ls = [l_ref[rows, :] for l_ref in l_scr]
        mx = jnp.maximum(jnp.maximum(ls[0], ls[1]), ls[2])
        es = [jnp.exp(x - mx) for x in ls]
        den = es[0] + es[1] + es[2]
        out_ref[rows, :] = sum((e / den) * o_ref[rows, :] for e, o_ref in zip(es, o_scr))
        return carry

    lax.fori_loop(0, seq // chunk, merge, 0)


def _dil_prompt(qg, k, v, slopes, batch, seq):
    d = k.shape[1]
    lanes = 2 * HEAD_DIM
    assert all(seq % (BLOCK * dil) == 0 for _, dil in BRANCHES) and (seq // BLOCK) % 2 == 0
    tok = pl.BlockSpec((None, seq, lanes), lambda b, h: (b, 0, h))
    out = pl.pallas_call(
        functools.partial(_dil_prompt_kernel, seq=seq),
        grid=(batch, d // lanes),
        in_specs=[pl.BlockSpec(memory_space=pltpu.SMEM)] + [tok] * 5,
        out_specs=tok,
        out_shape=jax.ShapeDtypeStruct((batch, seq, d), F32),
        scratch_shapes=[pltpu.VMEM((seq, lanes), F32)] * (2 * len(BRANCHES)),
        compiler_params=_params("parallel", "parallel"),
        name="dil_prompt",
    )(slopes.reshape(1, -1), *(a.reshape(batch, seq, d) for a in (*qg, k, v)))
    return out.reshape(batch * seq, d)


def _dil_sample_kernel(q_ref, slope_ref, sel_ref, knew_ref, vnew_ref, kt_ref, vt_ref, o_ref, *, t_new):
    n_buf = kt_ref.shape[1]
    rows = q_ref.shape[1]
    slope = slope_ref[:, 0:1]
    knew = knew_ref[...].astype(BF16)
    vnew = vnew_ref[...].astype(BF16)
    outs, lses, ls = [], [], []
    for g, (win, dil) in enumerate(BRANCHES):
        q = q_ref[g]
        t_q = lax.broadcasted_iota(jnp.int32, (rows, win), 0) % t_new
        dist = win + t_q - lax.broadcasted_iota(jnp.int32, (rows, win), 1)
        valid = jnp.logical_and(jnp.bitwise_and(dist, dil - 1) == 0, dist <= win)
        s = _dot(q, kt_ref[:, n_buf - win:].astype(BF16)) - slope * dist.astype(F32)
        s = jnp.where(valid, s, -jnp.inf)
        t_n = lax.broadcasted_iota(jnp.int32, (rows, LANES), 0) % t_new
        dist_n = t_n - lax.broadcasted_iota(jnp.int32, (rows, LANES), 1)
        valid_n = jnp.logical_and(jnp.bitwise_and(dist_n, dil - 1) == 0, dist_n >= 0)
        s_n = _dot(q, knew) - slope * dist_n.astype(F32)
        s_n = jnp.where(valid_n, s_n, -jnp.inf)
        m = jnp.maximum(jnp.max(s, axis=-1, keepdims=True), jnp.max(s_n, axis=-1, keepdims=True))
        p = jnp.exp(s - m)
        p_n = jnp.exp(s_n - m)
        l = jnp.sum(p, axis=-1, keepdims=True) + jnp.sum(p_n, axis=-1, keepdims=True)
        outs.append(_dot_nt(p.astype(BF16), vt_ref[:, n_buf - win:].astype(BF16)) + _dot_nt(p_n.astype(BF16), vnew))
        lses.append(m + jnp.log(l))
        ls.append(l)
    mx = jnp.maximum(jnp.maximum(lses[0], lses[1]), lses[2])
    es = [jnp.exp(x - mx) for x in lses]
    den = es[0] + es[1] + es[2]
    merged = sum((e / den / l) * o for e, l, o in zip(es, ls, outs))
    o_ref[...] = _diag_rows(merged, sel_ref[...], t_new)


def _dil_sample(qg, k_new, v_new, cache_k, cache_v, slopes, batch, t_new):
    _, n_buf, nh, hd = cache_k.shape
    d = nh * hd
    rows = nh * t_new
    assert n_buf >= BRANCHES[-1][0] and t_new <= min(dil for _, dil in BRANCHES[1:])
    q3 = jnp.stack([_row_queries(q, batch, t_new) for q in qg], axis=1)
    kt = cache_k.transpose(0, 2, 3, 1).reshape(batch, d, n_buf)
    vt = cache_v.transpose(0, 2, 3, 1).reshape(batch, d, n_buf)
    new = pl.BlockSpec((None, d, LANES), lambda b: (b, 0, 0))
    buf = pl.BlockSpec((None, d, n_buf), lambda b: (b, 0, 0))
    out = pl.pallas_call(
        functools.partial(_dil_sample_kernel, t_new=t_new),
        grid=(batch,),
        in_specs=[
            pl.BlockSpec((None, len(BRANCHES), rows, d), lambda b: (b, 0, 0, 0)),
            pl.BlockSpec((rows, LANES), lambda b: (0, 0)),
            pl.BlockSpec((OUT_ROWS, rows), lambda b: (0, 0)),
            new, new, buf, buf,
        ],
        out_specs=pl.BlockSpec((None, OUT_ROWS, d), lambda b: (b, 0, 0)),
        out_shape=jax.ShapeDtypeStruct((batch, OUT_ROWS, d), F32),
        compiler_params=_params("parallel"),
        name="dil_sample",
    )(q3, _row_heads(slopes, t_new), _select_rows(t_new), _new_keys_t(k_new, batch, t_new),
      _new_keys_t(v_new, batch, t_new), kt, vt)
    return out[:, :t_new].reshape(batch * t_new, d)


def _shift_kernel(ck_ref, k_new_ref, cv_ref, v_new_ref, ok_ref, ov_ref, *, t_new):
    for c_ref, new_ref, o_ref in ((ck_ref, k_new_ref, ok_ref), (cv_ref, v_new_ref, ov_ref)):
        o_ref[...] = jnp.concatenate([c_ref[:, t_new:], new_ref[:, :t_new]], axis=1)


def _shift_window(cache_k, cache_v, k_new, v_new, batch, t_new):
    _, n_buf, nh, hd = cache_k.shape
    d = nh * hd
    rows = 256
    kt = cache_k.transpose(0, 2, 3, 1).reshape(batch, d, n_buf)
    vt = cache_v.transpose(0, 2, 3, 1).reshape(batch, d, n_buf)
    buf = pl.BlockSpec((None, rows, n_buf), lambda b, j: (b, j, 0))
    new = pl.BlockSpec((None, rows, LANES), lambda b, j: (b, j, 0))
    ok, ov = pl.pallas_call(
        functools.partial(_shift_kernel, t_new=t_new),
        grid=(batch, d // rows),
        in_specs=[buf, new, buf, new],
        out_specs=[buf, buf],
        out_shape=[jax.ShapeDtypeStruct((batch, d, n_buf), cache_k.dtype)] * 2,
        compiler_params=_params("parallel", "parallel"),
        name="shift_window",
    )(kt, _new_keys_t(k_new, batch, t_new), vt, _new_keys_t(v_new, batch, t_new))
    back = lambda a: a.reshape(batch, nh, hd, n_buf).transpose(0, 3, 1, 2)
    return back(ok), back(ov)


def kernel(x_prompt, x_sample, cache_k_a, cache_v_a, cache_k_b, cache_v_b, page_table, g_pre_mix, g_post_mix, g_pre_mlp, g_post_mlp, w_qkv_a, w_o_a, b_sb, w_q_b, w_o_b, g_kv, w_kv_b, w_mlp_in, w_mlp_out):
    batch, seq, d = x_prompt.shape
    dec_b, t_new, _ = x_sample.shape
    depth = g_pre_mix.shape[0]
    n_a = w_qkv_a.shape[0]
    assert depth == 2 and n_a == 1 and seq <= BRANCHES[2][0] and d == N_HEADS * HEAD_DIM
    tm_p, tm_s = 512, dec_b * t_new
    slopes = _alibi_slopes(N_HEADS)
    bf = lambda w: w.astype(BF16)
    heads = (N_HEADS, HEAD_DIM)
    from_t = lambda a: a.reshape(a.shape[0], *heads, a.shape[-1]).transpose(0, 3, 1, 2)

    xp = x_prompt.reshape(batch * seq, d)
    xs = x_sample.reshape(dec_b * t_new, d)

    wqkv = bf(w_qkv_a[0])
    qp, kpt, vpt = _norm_matmul_t(xp, g_pre_mix[0], wqkv[:, :d], wqkv[:, d:].T, batch, seq, tm_p)
    qs, ks, vs = _norm_matmul(xs, g_pre_mix[0], wqkv, 3, tm_s)
    ap = _sb_prompt(qp, kpt, vpt, b_sb[0], batch, seq)
    as_ = _sb_sample(qs, ks, vs, cache_k_a, cache_v_a, page_table, b_sb[0], 0, dec_b, t_new)
    wo = bf(w_o_a[0])
    hp = _matmul_norm_res(ap, wo, g_post_mix[0], xp, tm_p)
    hs = _matmul_norm_res(as_, wo, g_post_mix[0], xs, tm_s)
    w_in, w_out = bf(w_mlp_in[0]), bf(w_mlp_out[0])
    hp = _mlp(hp, g_pre_mlp[0], w_in, w_out, g_post_mlp[0], 256)
    hs = _mlp(hs, g_pre_mlp[0], w_in, w_out, g_post_mlp[0], tm_s)

    wkv = bf(w_kv_b)
    kbp, vbp, kbpt, vbpt = _norm_matmul_t(hp, g_kv, wkv, wkv.T, batch, seq, tm_p)
    kbs, vbs = _norm_matmul(hs, g_kv, wkv, 2, tm_s)

    wq = bf(w_q_b[0])
    qgp = _norm_matmul(hp, g_pre_mix[1], wq, 3, tm_p)
    qgs = _norm_matmul(hs, g_pre_mix[1], wq, 3, tm_s)
    assert all(win // dil == BLOCK for win, dil in BRANCHES)
    mp = _dil_prompt(qgp, kbp, vbp, slopes, batch, seq)
    wo = bf(w_o_b[0])
    hp = _matmul_norm_res(mp, wo, g_post_mix[1], hp, tm_p)
    ms = _dil_sample(qgs, kbs, vbs, cache_k_b, cache_v_b, slopes, dec_b, t_new)
    hs = _matmul_norm_res(ms, wo, g_post_mix[1], hs, tm_s)
    w_in, w_out = bf(w_mlp_in[1]), bf(w_mlp_out[1])
    hp = _mlp(hp, g_pre_mlp[1], w_in, w_out, g_post_mlp[1], 256)
    hs = _mlp(hs, g_pre_mlp[1], w_in, w_out, g_post_mlp[1], tm_s)

    k_b_sample, v_b_sample = _shift_window(cache_k_b, cache_v_b, kbs, vbs, dec_b, t_new)
    keep = min(BRANCHES[-1][0], seq)
    return (
        hp.reshape(batch, seq, d),
        hs.reshape(dec_b, t_new, d),
        from_t(kpt)[None],
        from_t(vpt)[None],
        ks.reshape(1, dec_b, t_new, *heads),
        vs.reshape(1, dec_b, t_new, *heads),
        from_t(kbpt)[:, seq - keep:],
        from_t(vbpt)[:, seq - keep:],
        k_b_sample,
        v_b_sample,
    )
```

```python
import functools

import jax
import jax.numpy as jnp
from jax import lax
from jax.experimental import pallas as pl
from jax.experimental.pallas import tpu as pltpu

F32 = jnp.float32
BF16 = jnp.bfloat16

HEAD_DIM = 64
N_HEADS = 16
BRANCHES = ((128, 1), (512, 4), (2048, 16))
BLOCK = 128
LANES = 128
ATTN_SCALE = HEAD_DIM ** -0.5
RMS_EPS = 1e-6
VMEM_LIMIT = 56 * 1024 * 1024
OUT_ROWS = 16


def _params(*sem):
    return pltpu.CompilerParams(dimension_semantics=sem, vmem_limit_bytes=VMEM_LIMIT)


def _rms(x, g):
    return x * lax.rsqrt(jnp.mean(x * x, axis=-1, keepdims=True) + RMS_EPS) * g


def _dot(a, b):
    return jnp.dot(a, b, preferred_element_type=F32)


def _dot_nt(a, b):
    return lax.dot_general(a, b, (((1,), (1,)), ((), ())), preferred_element_type=F32)


def _split_bf16(x):
    hi = x.astype(BF16)
    return hi, (x - hi.astype(F32)).astype(BF16)


def _norm_matmul_kernel(x_ref, g_ref, w_ref, *out_refs):
    u = _rms(x_ref[...], g_ref[...]).astype(BF16)
    width = out_refs[0].shape[-1]
    for i, o_ref in enumerate(out_refs):
        o_ref[...] = _dot(u, w_ref[:, i * width:(i + 1) * width])


def _norm_matmul(x, g, w, n_out, tm):
    m, d = x.shape
    width = w.shape[1] // n_out
    return pl.pallas_call(
        _norm_matmul_kernel,
        grid=(m // tm,),
        in_specs=[
            pl.BlockSpec((tm, d), lambda i: (i, 0)),
            pl.BlockSpec((1, d), lambda i: (0, 0)),
            pl.BlockSpec(w.shape, lambda i: (0, 0)),
        ],
        out_specs=[pl.BlockSpec((tm, width), lambda i: (i, 0))] * n_out,
        out_shape=[jax.ShapeDtypeStruct((m, width), F32)] * n_out,
        compiler_params=_params("parallel"),
        name="norm_matmul",
    )(x, g.reshape(1, d), w)


def _norm_matmul_t_kernel(x_ref, g_ref, w_ref, wt_ref, *out_refs, n_row):
    u = _rms(x_ref[...], g_ref[...]).astype(BF16)
    width = x_ref.shape[-1]
    for i, o_ref in enumerate(out_refs[:n_row]):
        o_ref[...] = _dot(u, w_ref[:, i * width:(i + 1) * width])
    for i, o_ref in enumerate(out_refs[n_row:]):
        o_ref[...] = _dot_nt(wt_ref[i * width:(i + 1) * width, :], u)


def _norm_matmul_t(x, g, w, wt, batch, seq, tm):
    m, d = x.shape
    n_row, n_t = w.shape[1] // d, wt.shape[0] // d
    per_seq = seq // tm
    return pl.pallas_call(
        functools.partial(_norm_matmul_t_kernel, n_row=n_row),
        grid=(m // tm,),
        in_specs=[
            pl.BlockSpec((tm, d), lambda i: (i, 0)),
            pl.BlockSpec((1, d), lambda i: (0, 0)),
            pl.BlockSpec(w.shape, lambda i: (0, 0)),
            pl.BlockSpec(wt.shape, lambda i: (0, 0)),
        ],
        out_specs=[pl.BlockSpec((tm, d), lambda i: (i, 0))] * n_row
        + [pl.BlockSpec((None, d, tm), lambda i: (i // per_seq, 0, i % per_seq))] * n_t,
        out_shape=[jax.ShapeDtypeStruct((m, d), F32)] * n_row
        + [jax.ShapeDtypeStruct((batch, d, seq), F32)] * n_t,
        compiler_params=_params("parallel"),
        name="norm_matmul_t",
    )(x, g.reshape(1, d), w, wt)


def _matmul_norm_res_kernel(a_ref, w_ref, g_ref, h_ref, o_ref):
    y = _dot(a_ref[...].astype(BF16), w_ref[...])
    o_ref[...] = h_ref[...] + _rms(y, g_ref[...])


def _matmul_norm_res(a, w, g, h, tm):
    m, k = a.shape
    d = w.shape[1]
    return pl.pallas_call(
        _matmul_norm_res_kernel,
        grid=(m // tm,),
        in_specs=[
            pl.BlockSpec((tm, k), lambda i: (i, 0)),
            pl.BlockSpec(w.shape, lambda i: (0, 0)),
            pl.BlockSpec((1, d), lambda i: (0, 0)),
            pl.BlockSpec((tm, d), lambda i: (i, 0)),
        ],
        out_specs=pl.BlockSpec((tm, d), lambda i: (i, 0)),
        out_shape=jax.ShapeDtypeStruct((m, d), F32),
        compiler_params=_params("parallel"),
        name="matmul_norm_res",
    )(a, w, g.reshape(1, d), h)


def _mlp_kernel(h_ref, g1_ref, win_ref, wout_ref, g2_ref, o_ref, *, chunk):
    h = h_ref[...]
    u = _rms(h, g1_ref[...]).astype(BF16)
    acc = jnp.zeros(h.shape, F32)
    for c in range(win_ref.shape[1] // chunk):
        a = _dot(u, win_ref[:, c * chunk:(c + 1) * chunk])
        a = jnp.square(jnp.maximum(a, 0.0)).astype(BF16)
        acc = acc + _dot(a, wout_ref[c * chunk:(c + 1) * chunk, :])
    o_ref[...] = h + _rms(acc, g2_ref[...])


def _mlp(h, g1, w_in, w_out, g2, tm):
    m, d = h.shape
    return pl.pallas_call(
        functools.partial(_mlp_kernel, chunk=1024),
        grid=(m // tm,),
        in_specs=[
            pl.BlockSpec((tm, d), lambda i: (i, 0)),
            pl.BlockSpec((1, d), lambda i: (0, 0)),
            pl.BlockSpec(w_in.shape, lambda i: (0, 0)),
            pl.BlockSpec(w_out.shape, lambda i: (0, 0)),
            pl.BlockSpec((1, d), lambda i: (0, 0)),
        ],
        out_specs=pl.BlockSpec((tm, d), lambda i: (i, 0)),
        out_shape=jax.ShapeDtypeStruct((m, d), F32),
        compiler_params=_params("parallel"),
        name="mlp",
    )(h, g1.reshape(1, d), w_in, w_out, g2.reshape(1, d))


def _sb_tile(z, later, tail, mask):
    t = jnp.log(1.0 + jnp.exp(-jnp.abs(z)))
    ell = -jnp.maximum(z, 0.0) - t
    lsig = z + ell
    if mask is not None:
        ell = jnp.where(mask, ell, 0.0)
    rem = _dot(ell.astype(BF16), later) + tail
    a = jnp.exp(lsig + rem)
    if mask is not None:
        a = jnp.where(mask, a, 0.0)
    return a, jnp.sum(ell, axis=-1, keepdims=True)


def _later_matrix(n):
    row = lax.broadcasted_iota(jnp.int32, (n, n), 0)
    col = lax.broadcasted_iota(jnp.int32, (n, n), 1)
    return (row > col).astype(BF16)


def _sb_prompt_kernel(bias_ref, later_ref, q_ref, kt_ref, vt_ref, o_ref, *, tq):
    hp = pl.program_id(1)
    m = pl.program_id(2)
    wide = 2 * tq
    row = lax.broadcasted_iota(jnp.int32, (tq, wide), 0)
    col = lax.broadcasted_iota(jnp.int32, (tq, wide), 1)
    mask_lo = lax.broadcasted_iota(jnp.int32, (tq, tq), 1) < lax.broadcasted_iota(jnp.int32, (tq, tq), 0)
    mask_hi = col < row + tq
    later = later_ref[...]
    base = pl.multiple_of(m * wide, wide)
    heads = [(slice(hh * HEAD_DIM, (hh + 1) * HEAD_DIM), bias_ref[0, 2 * hp + hh]) for hh in range(2)]
    qs = [[(q_ref[part * tq:(part + 1) * tq, sl] * ATTN_SCALE).astype(BF16) for part in range(2)]
          for sl, _ in heads]

    def attend(qh, kth, vth, bias, later_kk, tail, acc, mask):
        a, ell_sum = _sb_tile(_dot(qh, kth) + bias, later_kk, tail, mask)
        return tail + ell_sum, acc + _dot_nt(a.astype(BF16), vth)

    zero = (jnp.zeros((tq, 1), F32), jnp.zeros((tq, HEAD_DIM), F32))
    carry = []
    for (sl, bias), (q_lo, q_hi) in zip(heads, qs):
        kth = kt_ref[sl, pl.ds(base, wide)].astype(BF16)
        vth = vt_ref[sl, pl.ds(base, wide)].astype(BF16)
        carry.append(attend(q_lo, kth[:, :tq], vth[:, :tq], bias, later[:tq, :tq], *zero, mask_lo))
        carry.append(attend(q_hi, kth, vth, bias, later, *zero, mask_hi))

    def sweep(jj, carry):
        start = pl.multiple_of((m - 1 - jj) * wide, wide)
        out = []
        for hh, ((sl, bias), q_parts) in enumerate(zip(heads, qs)):
            kth = kt_ref[sl, pl.ds(start, wide)].astype(BF16)
            vth = vt_ref[sl, pl.ds(start, wide)].astype(BF16)
            for part, qh in enumerate(q_parts):
                out.append(attend(qh, kth, vth, bias, later, *carry[2 * hh + part], None))
        return tuple(out)

    carry = lax.fori_loop(0, m, sweep, tuple(carry))
    for part in range(2):
        o_ref[part * tq:(part + 1) * tq, :] = jnp.concatenate([carry[part][1], carry[2 + part][1]], axis=-1)


def _sb_prompt(q, kt, vt, bias, batch, seq):
    d = q.shape[1]
    tq = 256
    wide = 2 * tq
    lanes = 2 * HEAD_DIM
    out = pl.pallas_call(
        functools.partial(_sb_prompt_kernel, tq=tq),
        grid=(batch, d // lanes, seq // wide),
        in_specs=[
            pl.BlockSpec(memory_space=pltpu.SMEM),
            pl.BlockSpec((wide, wide), lambda b, h, i: (0, 0)),
            pl.BlockSpec((None, wide, lanes), lambda b, h, i: (b, i, h)),
            pl.BlockSpec((None, lanes, seq), lambda b, h, i: (b, h, 0)),
            pl.BlockSpec((None, lanes, seq), lambda b, h, i: (b, h, 0)),
        ],
        out_specs=pl.BlockSpec((None, wide, lanes), lambda b, h, i: (b, i, h)),
        out_shape=jax.ShapeDtypeStruct((batch, seq, d), F32),
        compiler_params=_params("parallel", "parallel", "arbitrary"),
        name="sb_prompt",
    )(bias.reshape(1, -1), _later_matrix(wide), q.reshape(batch, seq, d), kt, vt)
    return out.reshape(batch * seq, d)


def _row_queries(q, batch, t_new):
    q4 = q.reshape(batch, t_new, N_HEADS, HEAD_DIM) * ATTN_SCALE
    eye = jnp.eye(N_HEADS, dtype=q.dtype)
    w = jnp.einsum('bthd,hg->bhtgd', q4, eye)
    return w.reshape(batch, N_HEADS * t_new, N_HEADS * HEAD_DIM).astype(BF16)


def _row_heads(x, t_new):
    return jnp.broadcast_to(jnp.repeat(x.astype(F32), t_new)[:, None], (N_HEADS * t_new, LANES))


def _new_keys_t(x, batch, t_new):
    xt = x.reshape(batch, t_new, N_HEADS * HEAD_DIM).transpose(0, 2, 1)
    return jnp.pad(xt, ((0, 0), (0, 0), (0, LANES - t_new)))


def _select_rows(t_new):
    r = jnp.arange(OUT_ROWS)[:, None]
    c = jnp.arange(N_HEADS * t_new)[None, :]
    return (c % t_new == r).astype(BF16)


def _diag_rows(acc, sel, t_new):
    row = lax.broadcasted_iota(jnp.int32, acc.shape, 0)
    col = lax.broadcasted_iota(jnp.int32, acc.shape, 1)
    own = jnp.where(row // t_new == col // HEAD_DIM, acc, 0.0)
    hi, lo = _split_bf16(own)
    return _dot(sel, hi) + _dot(sel, lo)


def _sb_sample_kernel(pt_ref, q_ref, bias_ref, later_ref, sel_ref, knew_ref, vnew_ref, *rest, pp, t_new):
    k_refs, v_refs = rest[:pp], rest[pp:2 * pp]
    o_ref, tail_ref, acc_ref = rest[2 * pp:]
    j = pl.program_id(1)
    q = q_ref[...]
    bias = bias_ref[:, 0:1]

    def block(kt, vt, later, mask):
        a, ell_sum = _sb_tile(_dot(q, kt) + bias, later, tail_ref[:, 0:1], mask)
        acc_ref[...] += _dot_nt(a.astype(BF16), vt)
        tail_ref[...] += ell_sum

    @pl.when(j == 0)
    def _():
        tail_ref[...] = jnp.zeros_like(tail_ref)
        acc_ref[...] = jnp.zeros_like(acc_ref)
        row = lax.broadcasted_iota(jnp.int32, (q.shape[0], LANES), 0)
        col = lax.broadcasted_iota(jnp.int32, (q.shape[0], LANES), 1)
        block(knew_ref[...].astype(BF16), vnew_ref[...].astype(BF16),
              later_ref[0:LANES, 0:LANES], col < row % t_new)

    kt = jnp.concatenate([r[...].astype(BF16) for r in k_refs], axis=1)
    vt = jnp.concatenate([r[...].astype(BF16) for r in v_refs], axis=1)
    block(kt, vt, later_ref[...], None)

    @pl.when(j == pl.num_programs(1) - 1)
    def _():
        o_ref[...] = _diag_rows(acc_ref[...], sel_ref[...], t_new)


def _sb_sample(q, k_new, v_new, cache_k, cache_v, page_table, bias, layer, batch, t_new):
    n_layers, n_phys, page, nh, hd = cache_k.shape
    d = nh * hd
    n_pages = page_table.shape[1]
    pp = 8
    rows = nh * t_new
    ck = cache_k.transpose(0, 1, 3, 4, 2).reshape(n_layers * n_phys, d, page)
    cv = cache_v.transpose(0, 1, 3, 4, 2).reshape(n_layers * n_phys, d, page)

    def page_spec(p):
        def index(b, j, pt):
            return (layer * n_phys + pt[b, n_pages - (j + 1) * pp + p], 0, 0)
        return pl.BlockSpec((None, d, page), index)

    const2 = lambda b, j, pt: (0, 0)
    per_b = lambda b, j, pt: (b, 0, 0)
    out = pl.pallas_call(
        functools.partial(_sb_sample_kernel, pp=pp, t_new=t_new),
        grid_spec=pltpu.PrefetchScalarGridSpec(
            num_scalar_prefetch=1,
            grid=(batch, n_pages // pp),
            in_specs=[
                pl.BlockSpec((None, rows, d), per_b),
                pl.BlockSpec((rows, LANES), const2),
                pl.BlockSpec((pp * page, pp * page), const2),
                pl.BlockSpec((OUT_ROWS, rows), const2),
                pl.BlockSpec((None, d, LANES), per_b),
                pl.BlockSpec((None, d, LANES), per_b),
            ] + [page_spec(p) for p in range(pp)] * 2,
            out_specs=pl.BlockSpec((None, OUT_ROWS, d), per_b),
            scratch_shapes=[pltpu.VMEM((rows, LANES), F32), pltpu.VMEM((rows, d), F32)],
        ),
        out_shape=jax.ShapeDtypeStruct((batch, OUT_ROWS, d), F32),
        compiler_params=_params("parallel", "arbitrary"),
        name="sb_sample",
    )(page_table, _row_queries(q, batch, t_new), _row_heads(bias, t_new), _later_matrix(pp * page),
      _select_rows(t_new), _new_keys_t(k_new, batch, t_new), _new_keys_t(v_new, batch, t_new),
      *([ck] * pp), *([cv] * pp))
    return out[:, :t_new].reshape(batch * t_new, d)


def _alibi_slopes(n_heads):
    return 2.0 ** (-8.0 * (jnp.arange(n_heads, dtype=F32) + 1.0) / n_heads)


def _dil_prompt_kernel(slope_ref, q0_ref, q1_ref, q2_ref, k_ref, v_ref, out_ref, *scratch, seq):
    o_scr, l_scr = scratch[:len(BRANCHES)], scratch[len(BRANCHES):]
    hp = pl.program_id(1)
    row = lax.broadcasted_iota(jnp.int32, (BLOCK, BLOCK), 0)
    col = lax.broadcasted_iota(jnp.int32, (BLOCK, BLOCK), 1)
    delta_c = row - col
    delta_p = delta_c + BLOCK
    valid_c = delta_c >= 0
    band_p = delta_p <= BLOCK
    heads = [(col // HEAD_DIM == hh, slope_ref[0, 2 * hp + hh]) for hh in range(2)]
    n_blocks = seq // BLOCK
    per_trip = 4
    for g, ((_, dil), q_ref) in enumerate(zip(BRANCHES, (q0_ref, q1_ref, q2_ref))):
        nb = seq // (dil * BLOCK)
        dist_c = (delta_c * dil).astype(F32)
        dist_p = (delta_p * dil).astype(F32)

        def rows_of(r, n, dil=dil):
            start = n * (BLOCK * dil) + r
            return pl.ds(start, BLOCK) if dil == 1 else pl.ds(start, BLOCK, stride=dil)

        def load(idx, q_ref=q_ref, nb=nb, rows_of=rows_of):
            r, n = idx // nb, idx % nb
            cur = rows_of(r, n)
            prev = rows_of(r, jnp.maximum(n - 1, 0))
            return cur, n, [q_ref[cur, :], k_ref[cur, :], v_ref[cur, :], k_ref[prev, :], v_ref[prev, :]]

        def attend(n, data, nb=nb, dist_c=dist_c, dist_p=dist_p):
            q = data[0] * ATTN_SCALE
            keys = jnp.concatenate([data[3].astype(BF16), data[1].astype(BF16)], axis=0)
            vals = jnp.concatenate([data[4].astype(BF16), data[2].astype(BF16)], axis=0)
            valid = jnp.concatenate([jnp.logical_and(band_p, n > 0), valid_c], axis=1)
            dist = jnp.concatenate([dist_p, dist_c], axis=1)
            outs, lses = [], []
            for own, slope in heads:
                qh = jnp.where(own, q, 0.0).astype(BF16)
                s = jnp.where(valid, _dot_nt(qh, keys) - slope * dist, -jnp.inf)
                m = jnp.max(s, axis=-1, keepdims=True)
                p = jnp.exp(s - m)
                l = jnp.sum(p, axis=-1, keepdims=True)
                outs.append(_dot(p.astype(BF16), vals) / l)
                lses.append(m + jnp.log(l))
            return jnp.where(heads[0][0], outs[0], outs[1]), jnp.where(heads[0][0], lses[0], lses[1])

        def blocks(i, carry, g=g, load=load, attend=attend):
            loaded = [load(i + u * (n_blocks // per_trip)) for u in range(per_trip)]
            results = [(cur, attend(n, data)) for cur, n, data in loaded]
            for cur, (o, lse) in results:
                o_scr[g][cur, :] = o
                l_scr[g][cur, :] = lse
            return carry

        lax.fori_loop(0, n_blocks // per_trip, blocks, 0)

    chunk = 2 * BLOCK

    def merge(c, carry):
        rows = pl.ds(pl.multiple_of(c * chunk, chunk), chunk)
        ls = [l_ref[rows, :] for l_ref in l_scr]
        mx = jnp.maximum(jnp.maximum(ls[0], ls[1]), ls[2])
        es = [jnp.exp(x - mx) for x in ls]
        den = es[0] + es[1] + es[2]
        out_ref[rows, :] = sum((e / den) * o_ref[rows, :] for e, o_ref in zip(es, o_scr))
        return carry

    lax.fori_loop(0, seq // chunk, merge, 0)


def _dil_prompt(qg, k, v, slopes, batch, seq):
    d = k.shape[1]
    lanes = 2 * HEAD_DIM
    assert all(seq % (BLOCK * dil) == 0 for _, dil in BRANCHES) and (seq // BLOCK) % 2 == 0
    tok = pl.BlockSpec((None, seq, lanes), lambda b, h: (b, 0, h))
    out = pl.pallas_call(
        functools.partial(_dil_prompt_kernel, seq=seq),
        grid=(batch, d // lanes),
        in_specs=[pl.BlockSpec(memory_space=pltpu.SMEM)] + [tok] * 5,
        out_specs=tok,
        out_shape=jax.ShapeDtypeStruct((batch, seq, d), F32),
        scratch_shapes=[pltpu.VMEM((seq, lanes), F32)] * (2 * len(BRANCHES)),
        compiler_params=_params("parallel", "parallel"),
        name="dil_prompt",
    )(slopes.reshape(1, -1), *(a.reshape(batch, seq, d) for a in (*qg, k, v)))
    return out.reshape(batch * seq, d)


def _dil_sample_kernel(q_ref, slope_ref, sel_ref, knew_ref, vnew_ref, kt_ref, vt_ref, o_ref, *, t_new):
    n_buf = kt_ref.shape[1]
    rows = q_ref.shape[1]
    slope = slope_ref[:, 0:1]
    knew = knew_ref[...].astype(BF16)
    vnew = vnew_ref[...].astype(BF16)
    outs, lses, ls = [], [], []
    for g, (win, dil) in enumerate(BRANCHES):
        q = q_ref[g]
        t_q = lax.broadcasted_iota(jnp.int32, (rows, win), 0) % t_new
        dist = win + t_q - lax.broadcasted_iota(jnp.int32, (rows, win), 1)
        valid = jnp.logical_and(jnp.bitwise_and(dist, dil - 1) == 0, dist <= win)
        s = _dot(q, kt_ref[:, n_buf - win:].astype(BF16)) - slope * dist.astype(F32)
        s = jnp.where(valid, s, -jnp.inf)
        t_n = lax.broadcasted_iota(jnp.int32, (rows, LANES), 0) % t_new
        dist_n = t_n - lax.broadcasted_iota(jnp.int32, (rows, LANES), 1)
        valid_n = jnp.logical_and(jnp.bitwise_and(dist_n, dil - 1) == 0, dist_n >= 0)
        s_n = _dot(q, knew) - slope * dist_n.astype(F32)
        s_n = jnp.where(valid_n, s_n, -jnp.inf)
        m = jnp.maximum(jnp.max(s, axis=-1, keepdims=True), jnp.max(s_n, axis=-1, keepdims=True))
        p = jnp.exp(s - m)
        p_n = jnp.exp(s_n - m)
        l = jnp.sum(p, axis=-1, keepdims=True) + jnp.sum(p_n, axis=-1, keepdims=True)
        outs.append(_dot_nt(p.astype(BF16), vt_ref[:, n_buf - win:].astype(BF16)) + _dot_nt(p_n.astype(BF16), vnew))
        lses.append(m + jnp.log(l))
        ls.append(l)
    mx = jnp.maximum(jnp.maximum(lses[0], lses[1]), lses[2])
    es = [jnp.exp(x - mx) for x in lses]
    den = es[0] + es[1] + es[2]
    merged = sum((e / den / l) * o for e, l, o in zip(es, ls, outs))
    o_ref[...] = _diag_rows(merged, sel_ref[...], t_new)


def _dil_sample(qg, k_new, v_new, cache_k, cache_v, slopes, batch, t_new):
    _, n_buf, nh, hd = cache_k.shape
    d = nh * hd
    rows = nh * t_new
    assert n_buf >= BRANCHES[-1][0] and t_new <= min(dil for _, dil in BRANCHES[1:])
    q3 = jnp.stack([_row_queries(q, batch, t_new) for q in qg], axis=1)
    kt = cache_k.transpose(0, 2, 3, 1).reshape(batch, d, n_buf)
    vt = cache_v.transpose(0, 2, 3, 1).reshape(batch, d, n_buf)
    new = pl.BlockSpec((None, d, LANES), lambda b: (b, 0, 0))
    buf = pl.BlockSpec((None, d, n_buf), lambda b: (b, 0, 0))
    out = pl.pallas_call(
        functools.partial(_dil_sample_kernel, t_new=t_new),
        grid=(batch,),
        in_specs=[
            pl.BlockSpec((None, len(BRANCHES), rows, d), lambda b: (b, 0, 0, 0)),
            pl.BlockSpec((rows, LANES), lambda b: (0, 0)),
            pl.BlockSpec((OUT_ROWS, rows), lambda b: (0, 0)),
            new, new, buf, buf,
        ],
        out_specs=pl.BlockSpec((None, OUT_ROWS, d), lambda b: (b, 0, 0)),
        out_shape=jax.ShapeDtypeStruct((batch, OUT_ROWS, d), F32),
        compiler_params=_params("parallel"),
        name="dil_sample",
    )(q3, _row_heads(slopes, t_new), _select_rows(t_new), _new_keys_t(k_new, batch, t_new),
      _new_keys_t(v_new, batch, t_new), kt, vt)
    return out[:, :t_new].reshape(batch * t_new, d)


def _shift_kernel(ck_ref, k_new_ref, cv_ref, v_new_ref, ok_ref, ov_ref, *, t_new):
    for c_ref, new_ref, o_ref in ((ck_ref, k_new_ref, ok_ref), (cv_ref, v_new_ref, ov_ref)):
        o_ref[...] = jnp.concatenate([c_ref[:, t_new:], new_ref[:, :t_new]], axis=1)


def _shift_window(cache_k, cache_v, k_new, v_new, batch, t_new):
    _, n_buf, nh, hd = cache_k.shape
    d = nh * hd
    rows = 256
    kt = cache_k.transpose(0, 2, 3, 1).reshape(batch, d, n_buf)
    vt = cache_v.transpose(0, 2, 3, 1).reshape(batch, d, n_buf)
    buf = pl.BlockSpec((None, rows, n_buf), lambda b, j: (b, j, 0))
    new = pl.BlockSpec((None, rows, LANES), lambda b, j: (b, j, 0))
    ok, ov = pl.pallas_call(
        functools.partial(_shift_kernel, t_new=t_new),
        grid=(batch, d // rows),
        in_specs=[buf, new, buf, new],
        out_specs=[buf, buf],
        out_shape=[jax.ShapeDtypeStruct((batch, d, n_buf), cache_k.dtype)] * 2,
        compiler_params=_params("parallel", "parallel"),
        name="shift_window",
    )(kt, _new_keys_t(k_new, batch, t_new), vt, _new_keys_t(v_new, batch, t_new))
    back = lambda a: a.reshape(batch, nh, hd, n_buf).transpose(0, 3, 1, 2)
    return back(ok), back(ov)


def kernel(x_prompt, x_sample, cache_k_a, cache_v_a, cache_k_b, cache_v_b, page_table, g_pre_mix, g_post_mix, g_pre_mlp, g_post_mlp, w_qkv_a, w_o_a, b_sb, w_q_b, w_o_b, g_kv, w_kv_b, w_mlp_in, w_mlp_out):
    batch, seq, d = x_prompt.shape
    dec_b, t_new, _ = x_sample.shape
    depth = g_pre_mix.shape[0]
    n_a = w_qkv_a.shape[0]
    assert depth == 2 and n_a == 1 and seq <= BRANCHES[2][0] and d == N_HEADS * HEAD_DIM
    tm_p, tm_s = 512, dec_b * t_new
    slopes = _alibi_slopes(N_HEADS)
    bf = lambda w: w.astype(BF16)
    heads = (N_HEADS, HEAD_DIM)
    from_t = lambda a: a.reshape(a.shape[0], *heads, a.shape[-1]).transpose(0, 3, 1, 2)

    xp = x_prompt.reshape(batch * seq, d)
    xs = x_sample.reshape(dec_b * t_new, d)

    wqkv = bf(w_qkv_a[0])
    qp, kpt, vpt = _norm_matmul_t(xp, g_pre_mix[0], wqkv[:, :d], wqkv[:, d:].T, batch, seq, tm_p)
    qs, ks, vs = _norm_matmul(xs, g_pre_mix[0], wqkv, 3, tm_s)
    ap = _sb_prompt(qp, kpt, vpt, b_sb[0], batch, seq)
    as_ = _sb_sample(qs, ks, vs, cache_k_a, cache_v_a, page_table, b_sb[0], 0, dec_b, t_new)
    wo = bf(w_o_a[0])
    hp = _matmul_norm_res(ap, wo, g_post_mix[0], xp, tm_p)
    hs = _matmul_norm_res(as_, wo, g_post_mix[0], xs, tm_s)
    w_in, w_out = bf(w_mlp_in[0]), bf(w_mlp_out[0])
    hp = _mlp(hp, g_pre_mlp[0], w_in, w_out, g_post_mlp[0], 256)
    hs = _mlp(hs, g_pre_mlp[0], w_in, w_out, g_post_mlp[0], tm_s)

    wkv = bf(w_kv_b)
    kbp, vbp, kbpt, vbpt = _norm_matmul_t(hp, g_kv, wkv, wkv.T, batch, seq, tm_p)
    kbs, vbs = _norm_matmul(hs, g_kv, wkv, 2, tm_s)

    wq = bf(w_q_b[0])
    qgp = _norm_matmul(hp, g_pre_mix[1], wq, 3, tm_p)
    qgs = _norm_matmul(hs, g_pre_mix[1], wq, 3, tm_s)
    assert all(win // dil == BLOCK for win, dil in BRANCHES)
    mp = _dil_prompt(qgp, kbp, vbp, slopes, batch, seq)
    wo = bf(w_o_b[0])
    hp = _matmul_norm_res(mp, wo, g_post_mix[1], hp, tm_p)
    ms = _dil_sample(qgs, kbs, vbs, cache_k_b, cache_v_b, slopes, dec_b, t_new)
    hs = _matmul_norm_res(ms, wo, g_post_mix[1], hs, tm_s)
    w_in, w_out = bf(w_mlp_in[1]), bf(w_mlp_out[1])
    hp = _mlp(hp, g_pre_mlp[1], w_in, w_out, g_post_mlp[1], 256)
    hs = _mlp(hs, g_pre_mlp[1], w_in, w_out, g_post_mlp[1], tm_s)

    k_b_sample, v_b_sample = _shift_window(cache_k_b, cache_v_b, kbs, vbs, dec_b, t_new)
    keep = min(BRANCHES[-1][0], seq)
    return (
        hp.reshape(batch, seq, d),
        hs.reshape(dec_b, t_new, d),
        from_t(kpt)[None],
        from_t(vpt)[None],
        ks.reshape(1, dec_b, t_new, *heads),
        vs.reshape(1, dec_b, t_new, *heads),
        from_t(kbpt)[:, seq - keep:],
        from_t(vbpt)[:, seq - keep:],
        k_b_sample,
        v_b_sample,
    )
```

```python
import functools

import jax
import jax.numpy as jnp
from jax import lax
from jax.experimental import pallas as pl
from jax.experimental.pallas import tpu as pltpu

F32 = jnp.float32
BF16 = jnp.bfloat16

HEAD_DIM = 64
N_HEADS = 16
BRANCHES = ((128, 1), (512, 4), (2048, 16))
BLOCK = 128
LANES = 128
ATTN_SCALE = HEAD_DIM ** -0.5
RMS_EPS = 1e-6
VMEM_LIMIT = 56 * 1024 * 1024
OUT_ROWS = 16


def _params(*sem):
    return pltpu.CompilerParams(dimension_semantics=sem, vmem_limit_bytes=VMEM_LIMIT)


def _rms(x, g):
    return x * lax.rsqrt(jnp.mean(x * x, axis=-1, keepdims=True) + RMS_EPS) * g


def _dot(a, b):
    return jnp.dot(a, b, preferred_element_type=F32)


def _dot_nt(a, b):
    return lax.dot_general(a, b, (((1,), (1,)), ((), ())), preferred_element_type=F32)


def _split_bf16(x):
    hi = x.astype(BF16)
    return hi, (x - hi.astype(F32)).astype(BF16)


def _norm_matmul_kernel(x_ref, g_ref, w_ref, *out_refs):
    u = _rms(x_ref[...], g_ref[...]).astype(BF16)
    width = out_refs[0].shape[-1]
    for i, o_ref in enumerate(out_refs):
        o_ref[...] = _dot(u, w_ref[:, i * width:(i + 1) * width])


def _norm_matmul(x, g, w, n_out, tm):
    m, d = x.shape
    width = w.shape[1] // n_out
    return pl.pallas_call(
        _norm_matmul_kernel,
        grid=(m // tm,),
        in_specs=[
            pl.BlockSpec((tm, d), lambda i: (i, 0)),
            pl.BlockSpec((1, d), lambda i: (0, 0)),
            pl.BlockSpec(w.shape, lambda i: (0, 0)),
        ],
        out_specs=[pl.BlockSpec((tm, width), lambda i: (i, 0))] * n_out,
        out_shape=[jax.ShapeDtypeStruct((m, width), F32)] * n_out,
        compiler_params=_params("parallel"),
        name="norm_matmul",
    )(x, g.reshape(1, d), w)


def _norm_matmul_t_kernel(x_ref, g_ref, w_ref, wt_ref, *out_refs, n_row):
    u = _rms(x_ref[...], g_ref[...]).astype(BF16)
    width = x_ref.shape[-1]
    for i, o_ref in enumerate(out_refs[:n_row]):
        o_ref[...] = _dot(u, w_ref[:, i * width:(i + 1) * width])
    for i, o_ref in enumerate(out_refs[n_row:]):
        o_ref[...] = _dot_nt(wt_ref[i * width:(i + 1) * width, :], u)


def _norm_matmul_t(x, g, w, wt, batch, seq, tm):
    m, d = x.shape
    n_row, n_t = w.shape[1] // d, wt.shape[0] // d
    per_seq = seq // tm
    return pl.pallas_call(
        functools.partial(_norm_matmul_t_kernel, n_row=n_row),
        grid=(m // tm,),
        in_specs=[
            pl.BlockSpec((tm, d), lambda i: (i, 0)),
            pl.BlockSpec((1, d), lambda i: (0, 0)),
            pl.BlockSpec(w.shape, lambda i: (0, 0)),
            pl.BlockSpec(wt.shape, lambda i: (0, 0)),
        ],
        out_specs=[pl.BlockSpec((tm, d), lambda i: (i, 0))] * n_row
        + [pl.BlockSpec((None, d, tm), lambda i: (i // per_seq, 0, i % per_seq))] * n_t,
        out_shape=[jax.ShapeDtypeStruct((m, d), F32)] * n_row
        + [jax.ShapeDtypeStruct((batch, d, seq), F32)] * n_t,
        compiler_params=_params("parallel"),
        name="norm_matmul_t",
    )(x, g.reshape(1, d), w, wt)


def _matmul_norm_res_kernel(a_ref, w_ref, g_ref, h_ref, o_ref):
    y = _dot(a_ref[...].astype(BF16), w_ref[...])
    o_ref[...] = h_ref[...] + _rms(y, g_ref[...])


def _matmul_norm_res(a, w, g, h, tm):
    m, k = a.shape
    d = w.shape[1]
    return pl.pallas_call(
        _matmul_norm_res_kernel,
        grid=(m // tm,),
        in_specs=[
            pl.BlockSpec((tm, k), lambda i: (i, 0)),
            pl.BlockSpec(w.shape, lambda i: (0, 0)),
            pl.BlockSpec((1, d), lambda i: (0, 0)),
            pl.BlockSpec((tm, d), lambda i: (i, 0)),
        ],
        out_specs=pl.BlockSpec((tm, d), lambda i: (i, 0)),
        out_shape=jax.ShapeDtypeStruct((m, d), F32),
        compiler_params=_params("parallel"),
        name="matmul_norm_res",
    )(a, w, g.reshape(1, d), h)


def _mlp_kernel(h_ref, g1_ref, win_ref, wout_ref, g2_ref, o_ref, *, chunk):
    h = h_ref[...]
    u = _rms(h, g1_ref[...]).astype(BF16)
    acc = jnp.zeros(h.shape, F32)
    for c in range(win_ref.shape[1] // chunk):
        a = _dot(u, win_ref[:, c * chunk:(c + 1) * chunk])
        a = jnp.square(jnp.maximum(a, 0.0)).astype(BF16)
        acc = acc + _dot(a, wout_ref[c * chunk:(c + 1) * chunk, :])
    o_ref[...] = h + _rms(acc, g2_ref[...])


def _mlp(h, g1, w_in, w_out, g2, tm):
    m, d = h.shape
    return pl.pallas_call(
        functools.partial(_mlp_kernel, chunk=1024),
        grid=(m // tm,),
        in_specs=[
            pl.BlockSpec((tm, d), lambda i: (i, 0)),
            pl.BlockSpec((1, d), lambda i: (0, 0)),
            pl.BlockSpec(w_in.shape, lambda i: (0, 0)),
            pl.BlockSpec(w_out.shape, lambda i: (0, 0)),
            pl.BlockSpec((1, d), lambda i: (0, 0)),
        ],
        out_specs=pl.BlockSpec((tm, d), lambda i: (i, 0)),
        out_shape=jax.ShapeDtypeStruct((m, d), F32),
        compiler_params=_params("parallel"),
        name="mlp",
    )(h, g1.reshape(1, d), w_in, w_out, g2.reshape(1, d))


def _sb_tile(z, later, tail, mask):
    t = jnp.log(1.0 + jnp.exp(-jnp.abs(z)))
    ell = -jnp.maximum(z, 0.0) - t
    lsig = z + ell
    if mask is not None:
        ell = jnp.where(mask, ell, 0.0)
    rem = _dot(ell.astype(BF16), later) + tail
    a = jnp.exp(lsig + rem)
    if mask is not None:
        a = jnp.where(mask, a, 0.0)
    return a, rem[:, 0:1] + ell[:, 0:1]


def _later_matrix(n):
    row = lax.broadcasted_iota(jnp.int32, (n, n), 0)
    col = lax.broadcasted_iota(jnp.int32, (n, n), 1)
    return (row > col).astype(BF16)


def _sb_prompt_kernel(bias_ref, later_ref, q_ref, kt_ref, vt_ref, o_ref, *, tq):
    hp = pl.program_id(1)
    m = pl.program_id(2)
    wide = 2 * tq
    row = lax.broadcasted_iota(jnp.int32, (tq, wide), 0)
    col = lax.broadcasted_iota(jnp.int32, (tq, wide), 1)
    mask_lo = lax.broadcasted_iota(jnp.int32, (tq, tq), 1) < lax.broadcasted_iota(jnp.int32, (tq, tq), 0)
    mask_hi = col < row + tq
    later = later_ref[...]
    base = pl.multiple_of(m * wide, wide)
    heads = [(slice(hh * HEAD_DIM, (hh + 1) * HEAD_DIM), bias_ref[0, 2 * hp + hh]) for hh in range(2)]
    qs = [[(q_ref[part * tq:(part + 1) * tq, sl] * ATTN_SCALE).astype(BF16) for part in range(2)]
          for sl, _ in heads]

    def attend(qh, kth, vth, bias, later_kk, tail, acc, mask):
        a, tail = _sb_tile(_dot(qh, kth) + bias, later_kk, tail, mask)
        return tail, acc + _dot_nt(a.astype(BF16), vth)

    zero = (jnp.zeros((tq, 1), F32), jnp.zeros((tq, HEAD_DIM), F32))
    carry = []
    for (sl, bias), (q_lo, q_hi) in zip(heads, qs):
        kth = kt_ref[sl, pl.ds(base, wide)].astype(BF16)
        vth = vt_ref[sl, pl.ds(base, wide)].astype(BF16)
        carry.append(attend(q_lo, kth[:, :tq], vth[:, :tq], bias, later[:tq, :tq], *zero, mask_lo))
        carry.append(attend(q_hi, kth, vth, bias, later, *zero, mask_hi))

    def sweep(jj, carry):
        start = pl.multiple_of((m - 1 - jj) * wide, wide)
        out = []
        for hh, ((sl, bias), q_parts) in enumerate(zip(heads, qs)):
            kth = kt_ref[sl, pl.ds(start, wide)].astype(BF16)
            vth = vt_ref[sl, pl.ds(start, wide)].astype(BF16)
            for part, qh in enumerate(q_parts):
                out.append(attend(qh, kth, vth, bias, later, *carry[2 * hh + part], None))
        return tuple(out)

    carry = lax.fori_loop(0, m, sweep, tuple(carry))
    for part in range(2):
        o_ref[part * tq:(part + 1) * tq, :] = jnp.concatenate([carry[part][1], carry[2 + part][1]], axis=-1)


def _sb_prompt(q, kt, vt, bias, batch, seq):
    d = q.shape[1]
    tq = 256
    wide = 2 * tq
    lanes = 2 * HEAD_DIM
    out = pl.pallas_call(
        functools.partial(_sb_prompt_kernel, tq=tq),
        grid=(batch, d // lanes, seq // wide),
        in_specs=[
            pl.BlockSpec(memory_space=pltpu.SMEM),
            pl.BlockSpec((wide, wide), lambda b, h, i: (0, 0)),
            pl.BlockSpec((None, wide, lanes), lambda b, h, i: (b, i, h)),
            pl.BlockSpec((None, lanes, seq), lambda b, h, i: (b, h, 0)),
            pl.BlockSpec((None, lanes, seq), lambda b, h, i: (b, h, 0)),
        ],
        out_specs=pl.BlockSpec((None, wide, lanes), lambda b, h, i: (b, i, h)),
        out_shape=jax.ShapeDtypeStruct((batch, seq, d), F32),
        compiler_params=_params("parallel", "parallel", "arbitrary"),
        name="sb_prompt",
    )(bias.reshape(1, -1), _later_matrix(wide), q.reshape(batch, seq, d), kt, vt)
    return out.reshape(batch * seq, d)


def _row_queries(q, batch, t_new):
    q4 = q.reshape(batch, t_new, N_HEADS, HEAD_DIM) * ATTN_SCALE
    eye = jnp.eye(N_HEADS, dtype=q.dtype)
    w = jnp.einsum('bthd,hg->bhtgd', q4, eye)
    return w.reshape(batch, N_HEADS * t_new, N_HEADS * HEAD_DIM).astype(BF16)


def _row_heads(x, t_new):
    return jnp.broadcast_to(jnp.repeat(x.astype(F32), t_new)[:, None], (N_HEADS * t_new, LANES))


def _new_keys_t(x, batch, t_new):
    xt = x.reshape(batch, t_new, N_HEADS * HEAD_DIM).transpose(0, 2, 1)
    return jnp.pad(xt, ((0, 0), (0, 0), (0, LANES - t_new)))


def _select_rows(t_new):
    r = jnp.arange(OUT_ROWS)[:, None]
    c = jnp.arange(N_HEADS * t_new)[None, :]
    return (c % t_new == r).astype(BF16)


def _diag_rows(acc, sel, t_new):
    row = lax.broadcasted_iota(jnp.int32, acc.shape, 0)
    col = lax.broadcasted_iota(jnp.int32, acc.shape, 1)
    own = jnp.where(row // t_new == col // HEAD_DIM, acc, 0.0)
    hi, lo = _split_bf16(own)
    return _dot(sel, hi) + _dot(sel, lo)


def _sb_sample_kernel(pt_ref, q_ref, bias_ref, later_ref, sel_ref, knew_ref, vnew_ref, *rest, pp, t_new):
    k_refs, v_refs = rest[:pp], rest[pp:2 * pp]
    o_ref, tail_ref, acc_ref = rest[2 * pp:]
    j = pl.program_id(1)
    q = q_ref[...]
    bias = bias_ref[:, 0:1]

    def block(kt, vt, later, mask):
        a, tail = _sb_tile(_dot(q, kt) + bias, later, tail_ref[:, 0:1], mask)
        acc_ref[...] += _dot_nt(a.astype(BF16), vt)
        tail_ref[...] = jnp.broadcast_to(tail, tail_ref.shape)

    @pl.when(j == 0)
    def _():
        tail_ref[...] = jnp.zeros_like(tail_ref)
        acc_ref[...] = jnp.zeros_like(acc_ref)
        row = lax.broadcasted_iota(jnp.int32, (q.shape[0], LANES), 0)
        col = lax.broadcasted_iota(jnp.int32, (q.shape[0], LANES), 1)
        block(knew_ref[...].astype(BF16), vnew_ref[...].astype(BF16),
              later_ref[0:LANES, 0:LANES], col < row % t_new)

    kt = jnp.concatenate([r[...].astype(BF16) for r in k_refs], axis=1)
    vt = jnp.concatenate([r[...].astype(BF16) for r in v_refs], axis=1)
    block(kt, vt, later_ref[...], None)

    @pl.when(j == pl.num_programs(1) - 1)
    def _():
        o_ref[...] = _diag_rows(acc_ref[...], sel_ref[...], t_new)


def _sb_sample(q, k_new, v_new, cache_k, cache_v, page_table, bias, layer, batch, t_new):
    n_layers, n_phys, page, nh, hd = cache_k.shape
    d = nh * hd
    n_pages = page_table.shape[1]
    pp = 8
    rows = nh * t_new
    ck = cache_k.transpose(0, 1, 3, 4, 2).reshape(n_layers * n_phys, d, page)
    cv = cache_v.transpose(0, 1, 3, 4, 2).reshape(n_layers * n_phys, d, page)

    def page_spec(p):
        def index(b, j, pt):
            return (layer * n_phys + pt[b, n_pages - (j + 1) * pp + p], 0, 0)
        return pl.BlockSpec((None, d, page), index)

    const2 = lambda b, j, pt: (0, 0)
    per_b = lambda b, j, pt: (b, 0, 0)
    out = pl.pallas_call(
        functools.partial(_sb_sample_kernel, pp=pp, t_new=t_new),
        grid_spec=pltpu.PrefetchScalarGridSpec(
            num_scalar_prefetch=1,
            grid=(batch, n_pages // pp),
            in_specs=[
                pl.BlockSpec((None, rows, d), per_b),
                pl.BlockSpec((rows, LANES), const2),
                pl.BlockSpec((pp * page, pp * page), const2),
                pl.BlockSpec((OUT_ROWS, rows), const2),
                pl.BlockSpec((None, d, LANES), per_b),
                pl.BlockSpec((None, d, LANES), per_b),
            ] + [page_spec(p) for p in range(pp)] * 2,
            out_specs=pl.BlockSpec((None, OUT_ROWS, d), per_b),
            scratch_shapes=[pltpu.VMEM((rows, LANES), F32), pltpu.VMEM((rows, d), F32)],
        ),
        out_shape=jax.ShapeDtypeStruct((batch, OUT_ROWS, d), F32),
        compiler_params=_params("parallel", "arbitrary"),
        name="sb_sample",
    )(page_table, _row_queries(q, batch, t_new), _row_heads(bias, t_new), _later_matrix(pp * page),
      _select_rows(t_new), _new_keys_t(k_new, batch, t_new), _new_keys_t(v_new, batch, t_new),
      *([ck] * pp), *([cv] * pp))
    return out[:, :t_new].reshape(batch * t_new, d)


def _alibi_slopes(n_heads):
    return 2.0 ** (-8.0 * (jnp.arange(n_heads, dtype=F32) + 1.0) / n_heads)


def _dil_prompt_kernel(slope_ref, q0_ref, q1_ref, q2_ref, k_ref, v_ref, out_ref, *scratch, seq):
    o_scr, l_scr = scratch[:len(BRANCHES)], scratch[len(BRANCHES):]
    hp = pl.program_id(1)
    row = lax.broadcasted_iota(jnp.int32, (BLOCK, BLOCK), 0)
    col = lax.broadcasted_iota(jnp.int32, (BLOCK, BLOCK), 1)
    delta_c = row - col
    delta_p = delta_c + BLOCK
    valid_c = delta_c >= 0
    band_p = delta_p <= BLOCK
    heads = [(col // HEAD_DIM == hh, slope_ref[0, 2 * hp + hh]) for hh in range(2)]
    n_blocks = seq // BLOCK
    per_trip = 4
    for g, ((_, dil), q_ref) in enumerate(zip(BRANCHES, (q0_ref, q1_ref, q2_ref))):
        nb = seq // (dil * BLOCK)
        dist_c = (delta_c * dil).astype(F32)
        dist_p = (delta_p * dil).astype(F32)

        def rows_of(r, n, dil=dil):
            start = n * (BLOCK * dil) + r
            return pl.ds(start, BLOCK) if dil == 1 else pl.ds(start, BLOCK, stride=dil)

        def load(idx, q_ref=q_ref, nb=nb, rows_of=rows_of):
            r, n = idx // nb, idx % nb
            cur = rows_of(r, n)
            prev = rows_of(r, jnp.maximum(n - 1, 0))
            return cur, n, [q_ref[cur, :], k_ref[cur, :], v_ref[cur, :], k_ref[prev, :], v_ref[prev, :]]

        def attend(n, data, nb=nb, dist_c=dist_c, dist_p=dist_p):
            q = data[0] * ATTN_SCALE
            keys = jnp.concatenate([data[3].astype(BF16), data[1].astype(BF16)], axis=0)
            vals = jnp.concatenate([data[4].astype(BF16), data[2].astype(BF16)], axis=0)
            valid = jnp.concatenate([jnp.logical_and(band_p, n > 0), valid_c], axis=1)
            dist = jnp.concatenate([dist_p, dist_c], axis=1)
            outs, lses = [], []
            for own, slope in heads:
                qh = jnp.where(own, q, 0.0).astype(BF16)
                s = jnp.where(valid, _dot_nt(qh, keys) - slope * dist, -jnp.inf)
                m = jnp.max(s, axis=-1, keepdims=True)
                p = jnp.exp(s - m)
                l = jnp.sum(p, axis=-1, keepdims=True)
                outs.append(_dot(p.astype(BF16), vals) / l)
                lses.append(m + jnp.log(l))
            return jnp.where(heads[0][0], outs[0], outs[1]), jnp.where(heads[0][0], lses[0], lses[1])

        def blocks(i, carry, g=g, load=load, attend=attend):
            loaded = [load(i + u * (n_blocks // per_trip)) for u in range(per_trip)]
            results = [(cur, attend(n, data)) for cur, n, data in loaded]
            for cur, (o, lse) in results:
                o_scr[g][cur, :] = o
                l_scr[g][cur, :] = lse
            return carry

        lax.fori_loop(0, n_blocks // per_trip, blocks, 0)

    chunk = 2 * BLOCK

    def merge(c, carry):
        rows = pl.ds(pl.multiple_of(c * chunk, chunk), chunk)
        ls = [l_ref[rows, :] for l_ref in l_scr]
        mx = jnp.maximum(jnp.maximum(ls[0], ls[1]), ls[2])
        es = [jnp.exp(x - mx) for x in ls]
        den = es[0] + es[1] + es[2]
        out_ref[rows, :] = sum((e / den) * o_ref[rows, :] for e, o_ref in zip(es, o_scr))
        return carry

    lax.fori_loop(0, seq // chunk, merge, 0)


def _dil_prompt(qg, k, v, slopes, batch, seq):
    d = k.shape[1]
    lanes = 2 * HEAD_DIM
    assert all(seq % (BLOCK * dil) == 0 for _, dil in BRANCHES) and (seq // BLOCK) % 2 == 0
    tok = pl.BlockSpec((None, seq, lanes), lambda b, h: (b, 0, h))
    out = pl.pallas_call(
        functools.partial(_dil_prompt_kernel, seq=seq),
        grid=(batch, d // lanes),
        in_specs=[pl.BlockSpec(memory_space=pltpu.SMEM)] + [tok] * 5,
        out_specs=tok,
        out_shape=jax.ShapeDtypeStruct((batch, seq, d), F32),
        scratch_shapes=[pltpu.VMEM((seq, lanes), F32)] * (2 * len(BRANCHES)),
        compiler_params=_params("parallel", "parallel"),
        name="dil_prompt",
    )(slopes.reshape(1, -1), *(a.reshape(batch, seq, d) for a in (*qg, k, v)))
    return out.reshape(batch * seq, d)


def _dil_sample_kernel(q_ref, slope_ref, sel_ref, knew_ref, vnew_ref, kt_ref, vt_ref, o_ref, *, t_new):
    n_buf = kt_ref.shape[1]
    rows = q_ref.shape[1]
    slope = slope_ref[:, 0:1]
    knew = knew_ref[...].astype(BF16)
    vnew = vnew_ref[...].astype(BF16)
    outs, lses, ls = [], [], []
    for g, (win, dil) in enumerate(BRANCHES):
        q = q_ref[g]
        t_q = lax.broadcasted_iota(jnp.int32, (rows, win), 0) % t_new
        dist = win + t_q - lax.broadcasted_iota(jnp.int32, (rows, win), 1)
        valid = jnp.logical_and(jnp.bitwise_and(dist, dil - 1) == 0, dist <= win)
        s = _dot(q, kt_ref[:, n_buf - win:].astype(BF16)) - slope * dist.astype(F32)
        s = jnp.where(valid, s, -jnp.inf)
        t_n = lax.broadcasted_iota(jnp.int32, (rows, LANES), 0) % t_new
        dist_n = t_n - lax.broadcasted_iota(jnp.int32, (rows, LANES), 1)
        valid_n = jnp.logical_and(jnp.bitwise_and(dist_n, dil - 1) == 0, dist_n >= 0)
        s_n = _dot(q, knew) - slope * dist_n.astype(F32)
        s_n = jnp.where(valid_n, s_n, -jnp.inf)
        m = jnp.maximum(jnp.max(s, axis=-1, keepdims=True), jnp.max(s_n, axis=-1, keepdims=True))
        p = jnp.exp(s - m)
        p_n = jnp.exp(s_n - m)
        l = jnp.sum(p, axis=-1, keepdims=True) + jnp.sum(p_n, axis=-1, keepdims=True)
        outs.append(_dot_nt(p.astype(BF16), vt_ref[:, n_buf - win:].astype(BF16)) + _dot_nt(p_n.astype(BF16), vnew))
        lses.append(m + jnp.log(l))
        ls.append(l)
    mx = jnp.maximum(jnp.maximum(lses[0], lses[1]), lses[2])
    es = [jnp.exp(x - mx) for x in lses]
    den = es[0] + es[1] + es[2]
    merged = sum((e / den / l) * o for e, l, o in zip(es, ls, outs))
    o_ref[...] = _diag_rows(merged, sel_ref[...], t_new)


def _dil_sample(qg, k_new, v_new, cache_k, cache_v, slopes, batch, t_new):
    _, n_buf, nh, hd = cache_k.shape
    d = nh * hd
    rows = nh * t_new
    assert n_buf >= BRANCHES[-1][0] and t_new <= min(dil for _, dil in BRANCHES[1:])
    q3 = jnp.stack([_row_queries(q, batch, t_new) for q in qg], axis=1)
    kt = cache_k.transpose(0, 2, 3, 1).reshape(batch, d, n_buf)
    vt = cache_v.transpose(0, 2, 3, 1).reshape(batch, d, n_buf)
    new = pl.BlockSpec((None, d, LANES), lambda b: (b, 0, 0))
    buf = pl.BlockSpec((None, d, n_buf), lambda b: (b, 0, 0))
    out = pl.pallas_call(
        functools.partial(_dil_sample_kernel, t_new=t_new),
        grid=(batch,),
        in_specs=[
            pl.BlockSpec((None, len(BRANCHES), rows, d), lambda b: (b, 0, 0, 0)),
            pl.BlockSpec((rows, LANES), lambda b: (0, 0)),
            pl.BlockSpec((OUT_ROWS, rows), lambda b: (0, 0)),
            new, new, buf, buf,
        ],
        out_specs=pl.BlockSpec((None, OUT_ROWS, d), lambda b: (b, 0, 0)),
        out_shape=jax.ShapeDtypeStruct((batch, OUT_ROWS, d), F32),
        compiler_params=_params("parallel"),
        name="dil_sample",
    )(q3, _row_heads(slopes, t_new), _select_rows(t_new), _new_keys_t(k_new, batch, t_new),
      _new_keys_t(v_new, batch, t_new), kt, vt)
    return out[:, :t_new].reshape(batch * t_new, d)


def _shift_kernel(ck_ref, k_new_ref, cv_ref, v_new_ref, ok_ref, ov_ref, *, t_new):
    for c_ref, new_ref, o_ref in ((ck_ref, k_new_ref, ok_ref), (cv_ref, v_new_ref, ov_ref)):
        o_ref[...] = jnp.concatenate([c_ref[:, t_new:], new_ref[:, :t_new]], axis=1)


def _shift_window(cache_k, cache_v, k_new, v_new, batch, t_new):
    _, n_buf, nh, hd = cache_k.shape
    d = nh * hd
    rows = 256
    kt = cache_k.transpose(0, 2, 3, 1).reshape(batch, d, n_buf)
    vt = cache_v.transpose(0, 2, 3, 1).reshape(batch, d, n_buf)
    buf = pl.BlockSpec((None, rows, n_buf), lambda b, j: (b, j, 0))
    new = pl.BlockSpec((None, rows, LANES), lambda b, j: (b, j, 0))
    ok, ov = pl.pallas_call(
        functools.partial(_shift_kernel, t_new=t_new),
        grid=(batch, d // rows),
        in_specs=[buf, new, buf, new],
        out_specs=[buf, buf],
        out_shape=[jax.ShapeDtypeStruct((batch, d, n_buf), cache_k.dtype)] * 2,
        compiler_params=_params("parallel", "parallel"),
        name="shift_window",
    )(kt, _new_keys_t(k_new, batch, t_new), vt, _new_keys_t(v_new, batch, t_new))
    back = lambda a: a.reshape(batch, nh, hd, n_buf).transpose(0, 3, 1, 2)
    return back(ok), back(ov)


def kernel(x_prompt, x_sample, cache_k_a, cache_v_a, cache_k_b, cache_v_b, page_table, g_pre_mix, g_post_mix, g_pre_mlp, g_post_mlp, w_qkv_a, w_o_a, b_sb, w_q_b, w_o_b, g_kv, w_kv_b, w_mlp_in, w_mlp_out):
    batch, seq, d = x_prompt.shape
    dec_b, t_new, _ = x_sample.shape
    depth = g_pre_mix.shape[0]
    n_a = w_qkv_a.shape[0]
    assert depth == 2 and n_a == 1 and seq <= BRANCHES[2][0] and d == N_HEADS * HEAD_DIM
    tm_p, tm_s = 512, dec_b * t_new
    slopes = _alibi_slopes(N_HEADS)
    bf = lambda w: w.astype(BF16)
    heads = (N_HEADS, HEAD_DIM)
    from_t = lambda a: a.reshape(a.shape[0], *heads, a.shape[-1]).transpose(0, 3, 1, 2)

    xp = x_prompt.reshape(batch * seq, d)
    xs = x_sample.reshape(dec_b * t_new, d)

    wqkv = bf(w_qkv_a[0])
    qp, kpt, vpt = _norm_matmul_t(xp, g_pre_mix[0], wqkv[:, :d], wqkv[:, d:].T, batch, seq, tm_p)
    qs, ks, vs = _norm_matmul(xs, g_pre_mix[0], wqkv, 3, tm_s)
    ap = _sb_prompt(qp, kpt, vpt, b_sb[0], batch, seq)
    as_ = _sb_sample(qs, ks, vs, cache_k_a, cache_v_a, page_table, b_sb[0], 0, dec_b, t_new)
    wo = bf(w_o_a[0])
    hp = _matmul_norm_res(ap, wo, g_post_mix[0], xp, tm_p)
    hs = _matmul_norm_res(as_, wo, g_post_mix[0], xs, tm_s)
    w_in, w_out = bf(w_mlp_in[0]), bf(w_mlp_out[0])
    hp = _mlp(hp, g_pre_mlp[0], w_in, w_out, g_post_mlp[0], 256)
    hs = _mlp(hs, g_pre_mlp[0], w_in, w_out, g_post_mlp[0], tm_s)

    wkv = bf(w_kv_b)
    kbp, vbp, kbpt, vbpt = _norm_matmul_t(hp, g_kv, wkv, wkv.T, batch, seq, tm_p)
    kbs, vbs = _norm_matmul(hs, g_kv, wkv, 2, tm_s)

    wq = bf(w_q_b[0])
    qgp = _norm_matmul(hp, g_pre_mix[1], wq, 3, tm_p)
    qgs = _norm_matmul(hs, g_pre_mix[1], wq, 3, tm_s)
    assert all(win // dil == BLOCK for win, dil in BRANCHES)
    mp = _dil_prompt(qgp, kbp, vbp, slopes, batch, seq)
    wo = bf(w_o_b[0])
    hp = _matmul_norm_res(mp, wo, g_post_mix[1], hp, tm_p)
    ms = _dil_sample(qgs, kbs, vbs, cache_k_b, cache_v_b, slopes, dec_b, t_new)
    hs = _matmul_norm_res(ms, wo, g_post_mix[1], hs, tm_s)
    w_in, w_out = bf(w_mlp_in[1]), bf(w_mlp_out[1])
    hp = _mlp(hp, g_pre_mlp[1], w_in, w_out, g_post_mlp[1], 256)
    hs = _mlp(hs, g_pre_mlp[1], w_in, w_out, g_post_mlp[1], tm_s)

    k_b_sample, v_b_sample = _shift_window(cache_k_b, cache_v_b, kbs, vbs, dec_b, t_new)
    keep = min(BRANCHES[-1][0], seq)
    return (
        hp.reshape(batch, seq, d),
        hs.reshape(dec_b, t_new, d),
        from_t(kpt)[None],
        from_t(vpt)[None],
        ks.reshape(1, dec_b, t_new, *heads),
        vs.reshape(1, dec_b, t_new, *heads),
        from_t(kbpt)[:, seq - keep:],
        from_t(vbpt)[:, seq - keep:],
        k_b_sample,
        v_b_sample,
    )
```

```python
import functools

import jax
import jax.numpy as jnp
from jax import lax
from jax.experimental import pallas as pl
from jax.experimental.pallas import tpu as pltpu

F32 = jnp.float32
BF16 = jnp.bfloat16

HEAD_DIM = 64
N_HEADS = 16
BRANCHES = ((128, 1), (512, 4), (2048, 16))
BLOCK = 128
LANES = 128
ATTN_SCALE = HEAD_DIM ** -0.5
RMS_EPS = 1e-6
VMEM_LIMIT = 56 * 1024 * 1024
OUT_ROWS = 16


def _params(*sem):
    return pltpu.CompilerParams(dimension_semantics=sem, vmem_limit_bytes=VMEM_LIMIT)


def _rms(x, g):
    return x * lax.rsqrt(jnp.mean(x * x, axis=-1, keepdims=True) + RMS_EPS) * g


def _dot(a, b):
    return jnp.dot(a, b, preferred_element_type=F32)


def _dot_nt(a, b):
    return lax.dot_general(a, b, (((1,), (1,)), ((), ())), preferred_element_type=F32)


def _split_bf16(x):
    hi = x.astype(BF16)
    return hi, (x - hi.astype(F32)).astype(BF16)


def _norm_matmul_kernel(x_ref, g_ref, w_ref, *out_refs):
    u = _rms(x_ref[...], g_ref[...]).astype(BF16)
    width = out_refs[0].shape[-1]
    for i, o_ref in enumerate(out_refs):
        o_ref[...] = _dot(u, w_ref[:, i * width:(i + 1) * width])


def _norm_matmul(x, g, w, n_out, tm):
    m, d = x.shape
    width = w.shape[1] // n_out
    return pl.pallas_call(
        _norm_matmul_kernel,
        grid=(m // tm,),
        in_specs=[
            pl.BlockSpec((tm, d), lambda i: (i, 0)),
            pl.BlockSpec((1, d), lambda i: (0, 0)),
            pl.BlockSpec(w.shape, lambda i: (0, 0)),
        ],
        out_specs=[pl.BlockSpec((tm, width), lambda i: (i, 0))] * n_out,
        out_shape=[jax.ShapeDtypeStruct((m, width), F32)] * n_out,
        compiler_params=_params("parallel"),
        name="norm_matmul",
    )(x, g.reshape(1, d), w)


def _norm_matmul_t_kernel(x_ref, g_ref, w_ref, wt_ref, *out_refs, n_row):
    u = _rms(x_ref[...], g_ref[...]).astype(BF16)
    width = x_ref.shape[-1]
    for i, o_ref in enumerate(out_refs[:n_row]):
        o_ref[...] = _dot(u, w_ref[:, i * width:(i + 1) * width])
    for i, o_ref in enumerate(out_refs[n_row:]):
        o_ref[...] = _dot_nt(wt_ref[i * width:(i + 1) * width, :], u)


def _norm_matmul_t(x, g, w, wt, batch, seq, tm):
    m, d = x.shape
    n_row, n_t = w.shape[1] // d, wt.shape[0] // d
    per_seq = seq // tm
    return pl.pallas_call(
        functools.partial(_norm_matmul_t_kernel, n_row=n_row),
        grid=(m // tm,),
        in_specs=[
            pl.BlockSpec((tm, d), lambda i: (i, 0)),
            pl.BlockSpec((1, d), lambda i: (0, 0)),
            pl.BlockSpec(w.shape, lambda i: (0, 0)),
            pl.BlockSpec(wt.shape, lambda i: (0, 0)),
        ],
        out_specs=[pl.BlockSpec((tm, d), lambda i: (i, 0))] * n_row
        + [pl.BlockSpec((None, d, tm), lambda i: (i // per_seq, 0, i % per_seq))] * n_t,
        out_shape=[jax.ShapeDtypeStruct((m, d), F32)] * n_row
        + [jax.ShapeDtypeStruct((batch, d, seq), F32)] * n_t,
        compiler_params=_params("parallel"),
        name="norm_matmul_t",
    )(x, g.reshape(1, d), w, wt)


def _matmul_norm_res_kernel(a_ref, w_ref, g_ref, h_ref, o_ref):
    y = _dot(a_ref[...].astype(BF16), w_ref[...])
    o_ref[...] = h_ref[...] + _rms(y, g_ref[...])


def _matmul_norm_res(a, w, g, h, tm):
    m, k = a.shape
    d = w.shape[1]
    return pl.pallas_call(
        _matmul_norm_res_kernel,
        grid=(m // tm,),
        in_specs=[
            pl.BlockSpec((tm, k), lambda i: (i, 0)),
            pl.BlockSpec(w.shape, lambda i: (0, 0)),
            pl.BlockSpec((1, d), lambda i: (0, 0)),
            pl.BlockSpec((tm, d), lambda i: (i, 0)),
        ],
        out_specs=pl.BlockSpec((tm, d), lambda i: (i, 0)),
        out_shape=jax.ShapeDtypeStruct((m, d), F32),
        compiler_params=_params("parallel"),
        name="matmul_norm_res",
    )(a, w, g.reshape(1, d), h)


def _mlp_kernel(h_ref, g1_ref, win_ref, wout_ref, g2_ref, o_ref, *, chunk):
    h = h_ref[...]
    u = _rms(h, g1_ref[...]).astype(BF16)
    acc = jnp.zeros(h.shape, F32)
    for c in range(win_ref.shape[1] // chunk):
        a = _dot(u, win_ref[:, c * chunk:(c + 1) * chunk])
        a = jnp.square(jnp.maximum(a, 0.0)).astype(BF16)
        acc = acc + _dot(a, wout_ref[c * chunk:(c + 1) * chunk, :])
    o_ref[...] = h + _rms(acc, g2_ref[...])


def _mlp(h, g1, w_in, w_out, g2, tm):
    m, d = h.shape
    return pl.pallas_call(
        functools.partial(_mlp_kernel, chunk=1024),
        grid=(m // tm,),
        in_specs=[
            pl.BlockSpec((tm, d), lambda i: (i, 0)),
            pl.BlockSpec((1, d), lambda i: (0, 0)),
            pl.BlockSpec(w_in.shape, lambda i: (0, 0), pipeline_mode=pl.Buffered(1)),
            pl.BlockSpec(w_out.shape, lambda i: (0, 0), pipeline_mode=pl.Buffered(1)),
            pl.BlockSpec((1, d), lambda i: (0, 0)),
        ],
        out_specs=pl.BlockSpec((tm, d), lambda i: (i, 0)),
        out_shape=jax.ShapeDtypeStruct((m, d), F32),
        compiler_params=_params("parallel"),
        name="mlp",
    )(h, g1.reshape(1, d), w_in, w_out, g2.reshape(1, d))


def _sb_tile(z, later, tail, mask):
    t = jnp.log(1.0 + jnp.exp(-jnp.abs(z)))
    ell = -jnp.maximum(z, 0.0) - t
    lsig = z + ell
    if mask is not None:
        ell = jnp.where(mask, ell, 0.0)
    rem = _dot(ell.astype(BF16), later) + tail
    a = jnp.exp(lsig + rem)
    if mask is not None:
        a = jnp.where(mask, a, 0.0)
    return a, rem[:, 0:1] + ell[:, 0:1]


def _later_matrix(n):
    row = lax.broadcasted_iota(jnp.int32, (n, n), 0)
    col = lax.broadcasted_iota(jnp.int32, (n, n), 1)
    return (row > col).astype(BF16)


def _sb_prompt_kernel(bias_ref, later_ref, q_ref, kt_ref, vt_ref, o_ref, *, tq):
    hp = pl.program_id(1)
    m = pl.program_id(2)
    wide = 2 * tq
    row = lax.broadcasted_iota(jnp.int32, (tq, wide), 0)
    col = lax.broadcasted_iota(jnp.int32, (tq, wide), 1)
    mask_lo = lax.broadcasted_iota(jnp.int32, (tq, tq), 1) < lax.broadcasted_iota(jnp.int32, (tq, tq), 0)
    mask_hi = col < row + tq
    later = later_ref[...]
    base = pl.multiple_of(m * wide, wide)
    heads = [(slice(hh * HEAD_DIM, (hh + 1) * HEAD_DIM), bias_ref[0, 2 * hp + hh]) for hh in range(2)]
    qs = [[(q_ref[part * tq:(part + 1) * tq, sl] * ATTN_SCALE).astype(BF16) for part in range(2)]
          for sl, _ in heads]

    def attend(qh, kth, vth, bias, later_kk, tail, acc, mask):
        a, tail = _sb_tile(_dot(qh, kth) + bias, later_kk, tail, mask)
        return tail, acc + _dot_nt(a.astype(BF16), vth)

    zero = (jnp.zeros((tq, 1), F32), jnp.zeros((tq, HEAD_DIM), F32))
    carry = []
    for (sl, bias), (q_lo, q_hi) in zip(heads, qs):
        kth = kt_ref[sl, pl.ds(base, wide)].astype(BF16)
        vth = vt_ref[sl, pl.ds(base, wide)].astype(BF16)
        carry.append(attend(q_lo, kth[:, :tq], vth[:, :tq], bias, later[:tq, :tq], *zero, mask_lo))
        carry.append(attend(q_hi, kth, vth, bias, later, *zero, mask_hi))

    def sweep(jj, carry):
        start = pl.multiple_of((m - 1 - jj) * wide, wide)
        out = []
        for hh, ((sl, bias), q_parts) in enumerate(zip(heads, qs)):
            kth = kt_ref[sl, pl.ds(start, wide)].astype(BF16)
            vth = vt_ref[sl, pl.ds(start, wide)].astype(BF16)
            for part, qh in enumerate(q_parts):
                out.append(attend(qh, kth, vth, bias, later, *carry[2 * hh + part], None))
        return tuple(out)

    carry = lax.fori_loop(0, m, sweep, tuple(carry))
    for part in range(2):
        o_ref[part * tq:(part + 1) * tq, :] = jnp.concatenate([carry[part][1], carry[2 + part][1]], axis=-1)


def _sb_prompt(q, kt, vt, bias, batch, seq):
    d = q.shape[1]
    tq = 256
    wide = 2 * tq
    lanes = 2 * HEAD_DIM
    out = pl.pallas_call(
        functools.partial(_sb_prompt_kernel, tq=tq),
        grid=(batch, d // lanes, seq // wide),
        in_specs=[
            pl.BlockSpec(memory_space=pltpu.SMEM),
            pl.BlockSpec((wide, wide), lambda b, h, i: (0, 0)),
            pl.BlockSpec((None, wide, lanes), lambda b, h, i: (b, i, h)),
            pl.BlockSpec((None, lanes, seq), lambda b, h, i: (b, h, 0)),
            pl.BlockSpec((None, lanes, seq), lambda b, h, i: (b, h, 0)),
        ],
        out_specs=pl.BlockSpec((None, wide, lanes), lambda b, h, i: (b, i, h)),
        out_shape=jax.ShapeDtypeStruct((batch, seq, d), F32),
        compiler_params=_params("parallel", "parallel", "arbitrary"),
        name="sb_prompt",
    )(bias.reshape(1, -1), _later_matrix(wide), q.reshape(batch, seq, d), kt, vt)
    return out.reshape(batch * seq, d)


def _row_queries(q, batch, t_new):
    q4 = q.reshape(batch, t_new, N_HEADS, HEAD_DIM) * ATTN_SCALE
    eye = jnp.eye(N_HEADS, dtype=q.dtype)
    w = jnp.einsum('bthd,hg->bhtgd', q4, eye)
    return w.reshape(batch, N_HEADS * t_new, N_HEADS * HEAD_DIM).astype(BF16)


def _row_heads(x, t_new):
    return jnp.broadcast_to(jnp.repeat(x.astype(F32), t_new)[:, None], (N_HEADS * t_new, LANES))


def _new_keys_t(x, batch, t_new):
    xt = x.reshape(batch, t_new, N_HEADS * HEAD_DIM).transpose(0, 2, 1)
    return jnp.pad(xt, ((0, 0), (0, 0), (0, LANES - t_new)))


def _select_rows(t_new):
    r = jnp.arange(OUT_ROWS)[:, None]
    c = jnp.arange(N_HEADS * t_new)[None, :]
    return (c % t_new == r).astype(BF16)


def _diag_rows(acc, sel, t_new):
    row = lax.broadcasted_iota(jnp.int32, acc.shape, 0)
    col = lax.broadcasted_iota(jnp.int32, acc.shape, 1)
    own = jnp.where(row // t_new == col // HEAD_DIM, acc, 0.0)
    hi, lo = _split_bf16(own)
    return _dot(sel, hi) + _dot(sel, lo)


def _sb_sample_kernel(pt_ref, q_ref, bias_ref, later_ref, sel_ref, knew_ref, vnew_ref, *rest, pp, t_new):
    k_refs, v_refs = rest[:pp], rest[pp:2 * pp]
    o_ref, tail_ref, acc_ref = rest[2 * pp:]
    j = pl.program_id(1)
    q = q_ref[...]
    bias = bias_ref[:, 0:1]

    def block(kt, vt, later, mask):
        a, tail = _sb_tile(_dot(q, kt) + bias, later, tail_ref[:, 0:1], mask)
        acc_ref[...] += _dot_nt(a.astype(BF16), vt)
        tail_ref[...] = jnp.broadcast_to(tail, tail_ref.shape)

    @pl.when(j == 0)
    def _():
        tail_ref[...] = jnp.zeros_like(tail_ref)
        acc_ref[...] = jnp.zeros_like(acc_ref)
        row = lax.broadcasted_iota(jnp.int32, (q.shape[0], LANES), 0)
        col = lax.broadcasted_iota(jnp.int32, (q.shape[0], LANES), 1)
        block(knew_ref[...].astype(BF16), vnew_ref[...].astype(BF16),
              later_ref[0:LANES, 0:LANES], col < row % t_new)

    kt = jnp.concatenate([r[...].astype(BF16) for r in k_refs], axis=1)
    vt = jnp.concatenate([r[...].astype(BF16) for r in v_refs], axis=1)
    block(kt, vt, later_ref[...], None)

    @pl.when(j == pl.num_programs(1) - 1)
    def _():
        o_ref[...] = _diag_rows(acc_ref[...], sel_ref[...], t_new)


def _sb_sample(q, k_new, v_new, cache_k, cache_v, page_table, bias, layer, batch, t_new):
    n_layers, n_phys, page, nh, hd = cache_k.shape
    d = nh * hd
    n_pages = page_table.shape[1]
    pp = 8
    rows = nh * t_new
    ck = cache_k.transpose(0, 1, 3, 4, 2).reshape(n_layers * n_phys, d, page)
    cv = cache_v.transpose(0, 1, 3, 4, 2).reshape(n_layers * n_phys, d, page)

    def page_spec(p):
        def index(b, j, pt):
            return (layer * n_phys + pt[b, n_pages - (j + 1) * pp + p], 0, 0)
        return pl.BlockSpec((None, d, page), index)

    const2 = lambda b, j, pt: (0, 0)
    per_b = lambda b, j, pt: (b, 0, 0)
    out = pl.pallas_call(
        functools.partial(_sb_sample_kernel, pp=pp, t_new=t_new),
        grid_spec=pltpu.PrefetchScalarGridSpec(
            num_scalar_prefetch=1,
            grid=(batch, n_pages // pp),
            in_specs=[
                pl.BlockSpec((None, rows, d), per_b),
                pl.BlockSpec((rows, LANES), const2),
                pl.BlockSpec((pp * page, pp * page), const2),
                pl.BlockSpec((OUT_ROWS, rows), const2),
                pl.BlockSpec((None, d, LANES), per_b),
                pl.BlockSpec((None, d, LANES), per_b),
            ] + [page_spec(p) for p in range(pp)] * 2,
            out_specs=pl.BlockSpec((None, OUT_ROWS, d), per_b),
            scratch_shapes=[pltpu.VMEM((rows, LANES), F32), pltpu.VMEM((rows, d), F32)],
        ),
        out_shape=jax.ShapeDtypeStruct((batch, OUT_ROWS, d), F32),
        compiler_params=_params("parallel", "arbitrary"),
        name="sb_sample",
    )(page_table, _row_queries(q, batch, t_new), _row_heads(bias, t_new), _later_matrix(pp * page),
      _select_rows(t_new), _new_keys_t(k_new, batch, t_new), _new_keys_t(v_new, batch, t_new),
      *([ck] * pp), *([cv] * pp))
    return out[:, :t_new].reshape(batch * t_new, d)


def _alibi_slopes(n_heads):
    return 2.0 ** (-8.0 * (jnp.arange(n_heads, dtype=F32) + 1.0) / n_heads)


def _dil_prompt_kernel(slope_ref, q0_ref, q1_ref, q2_ref, k_ref, v_ref, out_ref, *scratch, seq):
    o_scr, l_scr = scratch[:len(BRANCHES)], scratch[len(BRANCHES):]
    hp = pl.program_id(1)
    row = lax.broadcasted_iota(jnp.int32, (BLOCK, BLOCK), 0)
    col = lax.broadcasted_iota(jnp.int32, (BLOCK, BLOCK), 1)
    delta_c = row - col
    delta_p = delta_c + BLOCK
    valid_c = delta_c >= 0
    band_p = delta_p <= BLOCK
    heads = [(col // HEAD_DIM == hh, slope_ref[0, 2 * hp + hh]) for hh in range(2)]
    n_blocks = seq // BLOCK
    per_trip = 4
    for g, ((_, dil), q_ref) in enumerate(zip(BRANCHES, (q0_ref, q1_ref, q2_ref))):
        nb = seq // (dil * BLOCK)
        dist_c = (delta_c * dil).astype(F32)
        dist_p = (delta_p * dil).astype(F32)

        def rows_of(r, n, dil=dil):
            start = n * (BLOCK * dil) + r
            return pl.ds(start, BLOCK) if dil == 1 else pl.ds(start, BLOCK, stride=dil)

        def load(idx, q_ref=q_ref, nb=nb, rows_of=rows_of):
            r, n = idx // nb, idx % nb
            cur = rows_of(r, n)
            prev = rows_of(r, jnp.maximum(n - 1, 0))
            return cur, n, [q_ref[cur, :], k_ref[cur, :], v_ref[cur, :], k_ref[prev, :], v_ref[prev, :]]

        def attend(n, data, nb=nb, dist_c=dist_c, dist_p=dist_p):
            q = data[0] * ATTN_SCALE
            keys = jnp.concatenate([data[3].astype(BF16), data[1].astype(BF16)], axis=0)
            vals = jnp.concatenate([data[4].astype(BF16), data[2].astype(BF16)], axis=0)
            valid = jnp.concatenate([jnp.logical_and(band_p, n > 0), valid_c], axis=1)
            dist = jnp.concatenate([dist_p, dist_c], axis=1)
            outs, lses = [], []
            for own, slope in heads:
                qh = jnp.where(own, q, 0.0).astype(BF16)
                s = jnp.where(valid, _dot_nt(qh, keys) - slope * dist, -jnp.inf)
                m = jnp.max(s, axis=-1, keepdims=True)
                p = jnp.exp(s - m)
                l = jnp.sum(p, axis=-1, keepdims=True)
                outs.append(_dot(p.astype(BF16), vals) / l)
                lses.append(m + jnp.log(l))
            return jnp.where(heads[0][0], outs[0], outs[1]), jnp.where(heads[0][0], lses[0], lses[1])

        def blocks(i, carry, g=g, load=load, attend=attend):
            loaded = [load(i + u * (n_blocks // per_trip)) for u in range(per_trip)]
            results = [(cur, attend(n, data)) for cur, n, data in loaded]
            for cur, (o, lse) in results:
                o_scr[g][cur, :] = o
                l_scr[g][cur, :] = lse
            return carry

        lax.fori_loop(0, n_blocks // per_trip, blocks, 0)

    chunk = 2 * BLOCK

    def merge(c, carry):
        rows = pl.ds(pl.multiple_of(c * chunk, chunk), chunk)
        ls = [l_ref[rows, :] for l_ref in l_scr]
        mx = jnp.maximum(jnp.maximum(ls[0], ls[1]), ls[2])
        es = [jnp.exp(x - mx) for x in ls]
        den = es[0] + es[1] + es[2]
        out_ref[rows, :] = sum((e / den) * o_ref[rows, :] for e, o_ref in zip(es, o_scr))
        return carry

    lax.fori_loop(0, seq // chunk, merge, 0)


def _dil_prompt(qg, k, v, slopes, batch, seq):
    d = k.shape[1]
    lanes = 2 * HEAD_DIM
    assert all(seq % (BLOCK * dil) == 0 for _, dil in BRANCHES) and (seq // BLOCK) % 2 == 0
    tok = pl.BlockSpec((None, seq, lanes), lambda b, h: (b, 0, h))
    out = pl.pallas_call(
        functools.partial(_dil_prompt_kernel, seq=seq),
        grid=(batch, d // lanes),
        in_specs=[pl.BlockSpec(memory_space=pltpu.SMEM)] + [tok] * 5,
        out_specs=tok,
        out_shape=jax.ShapeDtypeStruct((batch, seq, d), F32),
        scratch_shapes=[pltpu.VMEM((seq, lanes), F32)] * (2 * len(BRANCHES)),
        compiler_params=_params("parallel", "parallel"),
        name="dil_prompt",
    )(slopes.reshape(1, -1), *(a.reshape(batch, seq, d) for a in (*qg, k, v)))
    return out.reshape(batch * seq, d)


def _dil_sample_kernel(q_ref, slope_ref, sel_ref, knew_ref, vnew_ref, kt_ref, vt_ref, o_ref, *, t_new):
    n_buf = kt_ref.shape[1]
    rows = q_ref.shape[1]
    slope = slope_ref[:, 0:1]
    knew = knew_ref[...].astype(BF16)
    vnew = vnew_ref[...].astype(BF16)
    outs, lses, ls = [], [], []
    for g, (win, dil) in enumerate(BRANCHES):
        q = q_ref[g]
        t_q = lax.broadcasted_iota(jnp.int32, (rows, win), 0) % t_new
        dist = win + t_q - lax.broadcasted_iota(jnp.int32, (rows, win), 1)
        valid = jnp.logical_and(jnp.bitwise_and(dist, dil - 1) == 0, dist <= win)
        s = _dot(q, kt_ref[:, n_buf - win:].astype(BF16)) - slope * dist.astype(F32)
        s = jnp.where(valid, s, -jnp.inf)
        t_n = lax.broadcasted_iota(jnp.int32, (rows, LANES), 0) % t_new
        dist_n = t_n - lax.broadcasted_iota(jnp.int32, (rows, LANES), 1)
        valid_n = jnp.logical_and(jnp.bitwise_and(dist_n, dil - 1) == 0, dist_n >= 0)
        s_n = _dot(q, knew) - slope * dist_n.astype(F32)
        s_n = jnp.where(valid_n, s_n, -jnp.inf)
        m = jnp.maximum(jnp.max(s, axis=-1, keepdims=True), jnp.max(s_n, axis=-1, keepdims=True))
        p = jnp.exp(s - m)
        p_n = jnp.exp(s_n - m)
        l = jnp.sum(p, axis=-1, keepdims=True) + jnp.sum(p_n, axis=-1, keepdims=True)
        outs.append(_dot_nt(p.astype(BF16), vt_ref[:, n_buf - win:].astype(BF16)) + _dot_nt(p_n.astype(BF16), vnew))
        lses.append(m + jnp.log(l))
        ls.append(l)
    mx = jnp.maximum(jnp.maximum(lses[0], lses[1]), lses[2])
    es = [jnp.exp(x - mx) for x in lses]
    den = es[0] + es[1] + es[2]
    merged = sum((e / den / l) * o for e, l, o in zip(es, ls, outs))
    o_ref[...] = _diag_rows(merged, sel_ref[...], t_new)


def _dil_sample(qg, k_new, v_new, cache_k, cache_v, slopes, batch, t_new):
    _, n_buf, nh, hd = cache_k.shape
    d = nh * hd
    rows = nh * t_new
    assert n_buf >= BRANCHES[-1][0] and t_new <= min(dil for _, dil in BRANCHES[1:])
    q3 = jnp.stack([_row_queries(q, batch, t_new) for q in qg], axis=1)
    kt = cache_k.transpose(0, 2, 3, 1).reshape(batch, d, n_buf)
    vt = cache_v.transpose(0, 2, 3, 1).reshape(batch, d, n_buf)
    new = pl.BlockSpec((None, d, LANES), lambda b: (b, 0, 0))
    buf = pl.BlockSpec((None, d, n_buf), lambda b: (b, 0, 0))
    out = pl.pallas_call(
        functools.partial(_dil_sample_kernel, t_new=t_new),
        grid=(batch,),
        in_specs=[
            pl.BlockSpec((None, len(BRANCHES), rows, d), lambda b: (b, 0, 0, 0)),
            pl.BlockSpec((rows, LANES), lambda b: (0, 0)),
            pl.BlockSpec((OUT_ROWS, rows), lambda b: (0, 0)),
            new, new, buf, buf,
        ],
        out_specs=pl.BlockSpec((None, OUT_ROWS, d), lambda b: (b, 0, 0)),
        out_shape=jax.ShapeDtypeStruct((batch, OUT_ROWS, d), F32),
        compiler_params=_params("parallel"),
        name="dil_sample",
    )(q3, _row_heads(slopes, t_new), _select_rows(t_new), _new_keys_t(k_new, batch, t_new),
      _new_keys_t(v_new, batch, t_new), kt, vt)
    return out[:, :t_new].reshape(batch * t_new, d)


def _shift_kernel(ck_ref, k_new_ref, cv_ref, v_new_ref, ok_ref, ov_ref, *, t_new):
    for c_ref, new_ref, o_ref in ((ck_ref, k_new_ref, ok_ref), (cv_ref, v_new_ref, ov_ref)):
        o_ref[...] = jnp.concatenate([c_ref[:, t_new:], new_ref[:, :t_new]], axis=1)


def _shift_window(cache_k, cache_v, k_new, v_new, batch, t_new):
    _, n_buf, nh, hd = cache_k.shape
    d = nh * hd
    rows = 256
    kt = cache_k.transpose(0, 2, 3, 1).reshape(batch, d, n_buf)
    vt = cache_v.transpose(0, 2, 3, 1).reshape(batch, d, n_buf)
    buf = pl.BlockSpec((None, rows, n_buf), lambda b, j: (b, j, 0))
    new = pl.BlockSpec((None, rows, LANES), lambda b, j: (b, j, 0))
    ok, ov = pl.pallas_call(
        functools.partial(_shift_kernel, t_new=t_new),
        grid=(batch, d // rows),
        in_specs=[buf, new, buf, new],
        out_specs=[buf, buf],
        out_shape=[jax.ShapeDtypeStruct((batch, d, n_buf), cache_k.dtype)] * 2,
        compiler_params=_params("parallel", "parallel"),
        name="shift_window",
    )(kt, _new_keys_t(k_new, batch, t_new), vt, _new_keys_t(v_new, batch, t_new))
    back = lambda a: a.reshape(batch, nh, hd, n_buf).transpose(0, 3, 1, 2)
    return back(ok), back(ov)


def kernel(x_prompt, x_sample, cache_k_a, cache_v_a, cache_k_b, cache_v_b, page_table, g_pre_mix, g_post_mix, g_pre_mlp, g_post_mlp, w_qkv_a, w_o_a, b_sb, w_q_b, w_o_b, g_kv, w_kv_b, w_mlp_in, w_mlp_out):
    batch, seq, d = x_prompt.shape
    dec_b, t_new, _ = x_sample.shape
    depth = g_pre_mix.shape[0]
    n_a = w_qkv_a.shape[0]
    assert depth == 2 and n_a == 1 and seq <= BRANCHES[2][0] and d == N_HEADS * HEAD_DIM
    tm_p, tm_s = 512, dec_b * t_new
    slopes = _alibi_slopes(N_HEADS)
    bf = lambda w: w.astype(BF16)
    heads = (N_HEADS, HEAD_DIM)
    from_t = lambda a: a.reshape(a.shape[0], *heads, a.shape[-1]).transpose(0, 3, 1, 2)

    xp = x_prompt.reshape(batch * seq, d)
    xs = x_sample.reshape(dec_b * t_new, d)

    wqkv = bf(w_qkv_a[0])
    qp, kpt, vpt = _norm_matmul_t(xp, g_pre_mix[0], wqkv[:, :d], wqkv[:, d:].T, batch, seq, tm_p)
    qs, ks, vs = _norm_matmul(xs, g_pre_mix[0], wqkv, 3, tm_s)
    ap = _sb_prompt(qp, kpt, vpt, b_sb[0], batch, seq)
    as_ = _sb_sample(qs, ks, vs, cache_k_a, cache_v_a, page_table, b_sb[0], 0, dec_b, t_new)
    wo = bf(w_o_a[0])
    hp = _matmul_norm_res(ap, wo, g_post_mix[0], xp, tm_p)
    hs = _matmul_norm_res(as_, wo, g_post_mix[0], xs, tm_s)
    w_in, w_out = bf(w_mlp_in[0]), bf(w_mlp_out[0])
    hp = _mlp(hp, g_pre_mlp[0], w_in, w_out, g_post_mlp[0], tm_p)
    hs = _mlp(hs, g_pre_mlp[0], w_in, w_out, g_post_mlp[0], tm_s)

    wkv = bf(w_kv_b)
    kbp, vbp, kbpt, vbpt = _norm_matmul_t(hp, g_kv, wkv, wkv.T, batch, seq, tm_p)
    kbs, vbs = _norm_matmul(hs, g_kv, wkv, 2, tm_s)

    wq = bf(w_q_b[0])
    qgp = _norm_matmul(hp, g_pre_mix[1], wq, 3, tm_p)
    qgs = _norm_matmul(hs, g_pre_mix[1], wq, 3, tm_s)
    assert all(win // dil == BLOCK for win, dil in BRANCHES)
    mp = _dil_prompt(qgp, kbp, vbp, slopes, batch, seq)
    wo = bf(w_o_b[0])
    hp = _matmul_norm_res(mp, wo, g_post_mix[1], hp, tm_p)
    ms = _dil_sample(qgs, kbs, vbs, cache_k_b, cache_v_b, slopes, dec_b, t_new)
    hs = _matmul_norm_res(ms, wo, g_post_mix[1], hs, tm_s)
    w_in, w_out = bf(w_mlp_in[1]), bf(w_mlp_out[1])
    hp = _mlp(hp, g_pre_mlp[1], w_in, w_out, g_post_mlp[1], tm_p)
    hs = _mlp(hs, g_pre_mlp[1], w_in, w_out, g_post_mlp[1], tm_s)

    k_b_sample, v_b_sample = _shift_window(cache_k_b, cache_v_b, kbs, vbs, dec_b, t_new)
    keep = min(BRANCHES[-1][0], seq)
    return (
        hp.reshape(batch, seq, d),
        hs.reshape(dec_b, t_new, d),
        from_t(kpt)[None],
        from_t(vpt)[None],
        ks.reshape(1, dec_b, t_new, *heads),
        vs.reshape(1, dec_b, t_new, *heads),
        from_t(kbpt)[:, seq - keep:],
        from_t(vbpt)[:, seq - keep:],
        k_b_sample,
        v_b_sample,
    )
```
